```python
import math, functools
import jax, jax.numpy as jnp
from jax import lax
import numpy as np

D_MODEL = 4096
BATCH = 4
SEQ = 4096
DEPTH = 1

CHUNK = 64
LEFT_CHUNKS = 8
BAND = LEFT_CHUNKS + 1
ATT_WIDTH = D_MODEL // 2
CONV_WIDTH = D_MODEL - ATT_WIDTH
N_HEADS = 16
HEAD_DIM = ATT_WIDTH // N_HEADS
MAX_REL = 256
CONV_K = 31
PLE_DIM = 256
EPS = 1e-6
NEG_INF = -1e30
IN_COLS = 4 * ATT_WIDTH + 3 * CONV_WIDTH

kernel_name = "hymba_chunked_attn_conformer_conv_block"


def rms_norm(x, g):
    xf = x.astype(jnp.float32)
    y = xf * lax.rsqrt(jnp.mean(xf * xf, axis=-1, keepdims=True) + EPS)
    return (y * g.astype(jnp.float32)).astype(x.dtype)


def layer_norm(x, g, b):
    xf = x.astype(jnp.float32)
    mu = jnp.mean(xf, axis=-1, keepdims=True)
    xc = xf - mu
    var = jnp.mean(xc * xc, axis=-1, keepdims=True)
    y = xc * lax.rsqrt(var + EPS) * g.astype(jnp.float32) + b.astype(jnp.float32)
    return y.astype(x.dtype)


def rel_bias(table):
    i = np.arange(CHUNK)[:, None, None]
    j = np.arange(BAND)[None, :, None]
    k = np.arange(CHUNK)[None, None, :]
    dist = (LEFT_CHUNKS - j) * CHUNK + i - k
    idx = np.clip(dist, -MAX_REL, MAX_REL) + MAX_REL
    return table[:, idx].astype(jnp.float32)


def chunked_attention(q, k, v, table):
    B, T, H, Dh = q.shape
    NC = T // CHUNK
    qc = (q * (Dh ** -0.5)).reshape(B, NC, CHUNK, H, Dh)
    pad = ((0, 0), (LEFT_CHUNKS * CHUNK, 0), (0, 0), (0, 0))
    kc = jnp.pad(k, pad).reshape(B, NC + LEFT_CHUNKS, CHUNK, H, Dh)
    vc = jnp.pad(v, pad).reshape(B, NC + LEFT_CHUNKS, CHUNK, H, Dh)
    s = jnp.stack(
        [jnp.einsum('bcqhd,bckhd->bhcqk', qc, kc[:, j:j + NC],
                    preferred_element_type=jnp.float32) for j in range(BAND)],
        axis=4)
    s = s + rel_bias(table)[None, :, None]
    valid = (np.arange(NC)[:, None] - LEFT_CHUNKS + np.arange(BAND)[None, :]) >= 0
    s = jnp.where(valid[None, None, :, None, :, None], s, NEG_INF)
    pr = jax.nn.softmax(s.reshape(B, H, NC, CHUNK, BAND * CHUNK), axis=-1)
    pr = pr.reshape(B, H, NC, CHUNK, BAND, CHUNK).astype(v.dtype)
    o = functools.reduce(
        lambda acc, t: acc + t,
        [jnp.einsum('bhcqk,bckhd->bcqhd', pr[:, :, :, :, j], vc[:, j:j + NC]) for j in range(BAND)])
    return o.reshape(B, T, H * Dh)


def conv_module(a, g, w_dw, b_dw, ln_g, ln_b, w_pw, b_pw):
    u = a * jax.nn.sigmoid(g)
    u = jnp.pad(u, ((0, 0), (CONV_K - 1, 0), (0, 0)))
    u = lax.conv_general_dilated(
        u, w_dw[:, None, :], window_strides=(1,), padding='VALID',
        dimension_numbers=('NWC', 'WIO', 'NWC'),
        feature_group_count=CONV_WIDTH) + b_dw
    u = jax.nn.silu(layer_norm(u, ln_g, ln_b))
    return u @ w_pw + b_pw


def setup_inputs(seed: int = 0) -> dict:
    key = jax.random.key(seed)
    ks = jax.random.split(key, 20)
    n = jax.random.normal
    f = jnp.float32
    return {
        "x": n(ks[0], (BATCH, SEQ, D_MODEL), f),
        "p": n(ks[1], (DEPTH, BATCH, SEQ, PLE_DIM), f),
        "norm_in_g": 1.0 + 0.05 * n(ks[2], (DEPTH, D_MODEL), f),
        "w_in": n(ks[3], (DEPTH, D_MODEL, IN_COLS), f) * D_MODEL ** -0.5,
        "rel_table": 0.1 * n(ks[4], (DEPTH, N_HEADS, 2 * MAX_REL + 1), f),
        "w_dw": n(ks[5], (DEPTH, CONV_K, CONV_WIDTH), f) * CONV_K ** -0.5,
        "b_dw": 0.02 * n(ks[6], (DEPTH, CONV_WIDTH), f),
        "conv_ln_g": 1.0 + 0.05 * n(ks[7], (DEPTH, CONV_WIDTH), f),
        "conv_ln_b": 0.02 * n(ks[8], (DEPTH, CONV_WIDTH), f),
        "w_pw": n(ks[9], (DEPTH, CONV_WIDTH, CONV_WIDTH), f) * CONV_WIDTH ** -0.5,
        "b_pw": 0.02 * n(ks[10], (DEPTH, CONV_WIDTH), f),
        "attn_out_g": 1.0 + 0.05 * n(ks[11], (DEPTH, ATT_WIDTH), f),
        "conv_out_g": 1.0 + 0.05 * n(ks[12], (DEPTH, CONV_WIDTH), f),
        "w_out": n(ks[13], (DEPTH, D_MODEL, D_MODEL), f) * D_MODEL ** -0.5,
        "ple_norm_g": 1.0 + 0.05 * n(ks[14], (DEPTH, D_MODEL), f),
        "w_ple_gate": n(ks[15], (DEPTH, D_MODEL, D_MODEL), f) * D_MODEL ** -0.5,
        "b_ple_gate": 0.02 * n(ks[16], (DEPTH, D_MODEL), f),
        "w_ple": n(ks[17], (DEPTH, PLE_DIM, D_MODEL), f) * PLE_DIM ** -0.5,
        "final_g": 1.0 + 0.05 * n(ks[18], (D_MODEL,), f),
    }


def reference(x, p, norm_in_g, w_in, rel_table, w_dw, b_dw, conv_ln_g, conv_ln_b,
              w_pw, b_pw, attn_out_g, conv_out_g, w_out, ple_norm_g, w_ple_gate,
              b_ple_gate, w_ple, final_g):
    B, T, _ = x.shape
    h = x
    for i in range(DEPTH):
        xn = rms_norm(h, norm_in_g[i])
        proj = xn @ w_in[i]
        q, k, v, z_a, a_c, g_c, z_c = jnp.split(
            proj, np.cumsum([ATT_WIDTH] * 4 + [CONV_WIDTH] * 2)[:6].tolist(), axis=-1)
        shp = (B, T, N_HEADS, HEAD_DIM)
        y_a = chunked_attention(q.reshape(shp), k.reshape(shp), v.reshape(shp), rel_table[i])
        y_a = rms_norm(y_a, attn_out_g[i]) * jax.nn.silu(z_a)
        y_c = conv_module(a_c, g_c, w_dw[i], b_dw[i], conv_ln_g[i], conv_ln_b[i], w_pw[i], b_pw[i])
        y_c = rms_norm(y_c, conv_out_g[i]) * jax.nn.silu(z_c)
        h = h + jnp.concatenate([y_a, y_c], axis=-1) @ w_out[i]
        gate = jax.nn.sigmoid(rms_norm(h, ple_norm_g[i]) @ w_ple_gate[i] + b_ple_gate[i])
        h = h + gate * (p[i] @ w_ple[i])
    return rms_norm(h, final_g)
```

```python
import functools

import numpy as np
import jax
import jax.numpy as jnp
from jax import lax
from jax.experimental import pallas as pl
from jax.experimental.pallas import tpu as pltpu

CHUNK = 64
LEFT_CHUNKS = 8
N_HEADS = 16
MAX_REL = 256
CONV_K = 31
EPS = 1e-6
NEG_INF = -1e30

Q_TILE = 256
KV_BLOCKS = 3
HALO = 32
VMEM_LIMIT = 56 * 1024 * 1024

F32 = jnp.float32
BF16 = jnp.bfloat16


def _cparams(n_axes):
    return pltpu.CompilerParams(
        dimension_semantics=("arbitrary",) * n_axes, vmem_limit_bytes=VMEM_LIMIT)


def _proj_kernel(x_ref, g_ref, w_ref, o_ref, xn_ref, *, row_chunk):
    tm = x_ref.shape[0]

    @pl.when(pl.program_id(1) == 0)
    def _():
        def body(r, carry):
            rows = pl.ds(pl.multiple_of(r * row_chunk, row_chunk), row_chunk)
            x = x_ref[rows, :]
            ms = jnp.mean(x * x, axis=-1, keepdims=True)
            xn_ref[rows, :] = (x * lax.rsqrt(ms + EPS) * g_ref[...]).astype(BF16)
            return carry
        lax.fori_loop(0, tm // row_chunk, body, 0)

    o_ref[...] = jnp.dot(xn_ref[...], w_ref[...], preferred_element_type=F32).astype(o_ref.dtype)


def _in_projection(x2, g, w_bf, *, tm, tn):
    m, d = x2.shape
    n = w_bf.shape[1]
    return pl.pallas_call(
        functools.partial(_proj_kernel, row_chunk=64),
        out_shape=jax.ShapeDtypeStruct((m, n), BF16),
        grid=(m // tm, n // tn),
        in_specs=[
            pl.BlockSpec((tm, d), lambda i, j: (i, 0)),
            pl.BlockSpec((1, d), lambda i, j: (0, 0)),
            pl.BlockSpec((d, tn), lambda i, j: (0, j)),
        ],
        out_specs=pl.BlockSpec((tm, tn), lambda i, j: (i, j)),
        scratch_shapes=[pltpu.VMEM((tm, d), BF16)],
        compiler_params=_cparams(2),
        name="in_projection",
    )(x2, g.reshape(1, d), w_bf)


def _attn_kernel(q_ref, k0_ref, k1_ref, k2_ref, v0_ref, v1_ref, v2_ref, z_ref, bias_ref, g_ref,
                 o_ref, y_ref, *, n_heads, head_dim):
    t = pl.program_id(1)
    scale = head_dim ** -0.5
    k_refs = (k0_ref, k1_ref, k2_ref)
    v_refs = (v0_ref, v1_ref, v2_ref)

    def head(h, carry):
        cols = pl.ds(pl.multiple_of(h * head_dim, head_dim), head_dim)
        q = q_ref[:, cols]
        s = []
        for j in range(KV_BLOCKS):
            sj = lax.dot_general(q, k_refs[j][:, cols], (((1,), (1,)), ((), ())),
                                 preferred_element_type=F32)
            sj = sj * scale + bias_ref[h, :, j * Q_TILE:(j + 1) * Q_TILE]
            if j < KV_BLOCKS - 1:
                sj = jnp.where(t + (j - (KV_BLOCKS - 1)) >= 0, sj, NEG_INF)
            s.append(sj)
        m = jnp.max(jnp.maximum(jnp.maximum(s[0], s[1]), s[2]), axis=-1, keepdims=True)
        p = [jnp.exp(sj - m) for sj in s]
        l = jnp.sum(p[0] + p[1] + p[2], axis=-1, keepdims=True)
        o = jnp.dot(p[0].astype(BF16), v_refs[0][:, cols], preferred_element_type=F32)
        o += jnp.dot(p[1].astype(BF16), v_refs[1][:, cols], preferred_element_type=F32)
        o += jnp.dot(p[2].astype(BF16), v_refs[2][:, cols], preferred_element_type=F32)
        y_ref[:, cols] = o / l
        return carry

    lax.fori_loop(0, n_heads, head, 0)

    y = y_ref[...]
    ms = jnp.mean(y * y, axis=-1, keepdims=True)
    z = z_ref[...].astype(F32)
    o_ref[...] = (y * lax.rsqrt(ms + EPS) * g_ref[...] * (z * jax.nn.sigmoid(z))).astype(o_ref.dtype)


def _band_bias(table):
    i = np.arange(Q_TILE)[:, None]
    m = np.arange(KV_BLOCKS * Q_TILE)[None, :]
    idx = np.clip((KV_BLOCKS - 1) * Q_TILE + i - m, -MAX_REL, MAX_REL) + MAX_REL
    rel_chunk = m // CHUNK - i // CHUNK
    in_band = (rel_chunk >= 0) & (rel_chunk <= LEFT_CHUNKS)
    return jnp.where(in_band[None], table[:, idx].astype(F32), NEG_INF)


def _attention(proj, bias, g, *, batch, seq, width, n_heads):
    m = proj.shape[0]
    tiles = seq // Q_TILE
    head_dim = width // n_heads

    def rows(off):
        def index_map(b, t, *, col):
            return (b * tiles + jnp.maximum(t + off, 0), col)
        return index_map

    def spec(off, col):
        return pl.BlockSpec((Q_TILE, width), functools.partial(rows(off), col=col))

    in_specs = [spec(0, 0),
                spec(-2, 1), spec(-1, 1), spec(0, 1),
                spec(-2, 2), spec(-1, 2), spec(0, 2),
                spec(0, 3),
                pl.BlockSpec(bias.shape, lambda b, t: (0, 0, 0)),
                pl.BlockSpec((1, width), lambda b, t: (0, 0))]
    return pl.pallas_call(
        functools.partial(_attn_kernel, n_heads=n_heads, head_dim=head_dim),
        out_shape=jax.ShapeDtypeStruct((m, width), BF16),
        grid=(batch, tiles),
        in_specs=in_specs,
        out_specs=pl.BlockSpec((Q_TILE, width), lambda b, t: (b * tiles + t, 0)),
        scratch_shapes=[pltpu.VMEM((Q_TILE, width), F32)],
        compiler_params=_cparams(2),
        name="band_attention",
    )(*([proj] * 8), bias, g.reshape(1, width))


def _conv_kernel(a_ref, gl_ref, z_ref, wdw_ref, bdw_ref, lng_ref, lnb_ref, wpw_ref, bpw_ref, g_ref,
                 o_ref, u_ref, c_ref, *, strip_rows, strip_cols):
    tt, width = a_ref.shape
    t = pl.program_id(1)

    @pl.when(t == 0)
    def _():
        u_ref[0:HALO, :] = jnp.zeros((HALO, width), F32)

    @pl.when(t > 0)
    def _():
        u_ref[0:HALO, :] = u_ref[tt:tt + HALO, :]

    a = a_ref[...].astype(F32)
    gl = gl_ref[...].astype(F32)
    u_ref[HALO:HALO + tt, :] = a * jax.nn.sigmoid(gl)

    def strip(idx, carry):
        cols = pl.ds(pl.multiple_of(idx * strip_cols, strip_cols), strip_cols)
        for r0 in range(0, tt, strip_rows):
            acc = jnp.zeros((strip_rows, strip_cols), F32)
            for k in range(CONV_K):
                row = r0 + (HALO - (CONV_K - 1)) + k
                acc += u_ref[row:row + strip_rows, cols] * wdw_ref[k:k + 1, cols]
            c_ref[r0:r0 + strip_rows, cols] = acc + bdw_ref[:, cols]
        return carry

    lax.fori_loop(0, width // strip_cols, strip, 0)

    c = c_ref[...]
    mu = jnp.mean(c, axis=-1, keepdims=True)
    cc = c - mu
    var = jnp.mean(cc * cc, axis=-1, keepdims=True)
    y = cc * lax.rsqrt(var + EPS) * lng_ref[...] + lnb_ref[...]
    y = y * jax.nn.sigmoid(y)
    pw = jnp.dot(y.astype(BF16), wpw_ref[...], preferred_element_type=F32) + bpw_ref[...]
    ms = jnp.mean(pw * pw, axis=-1, keepdims=True)
    z = z_ref[...].astype(F32)
    o_ref[...] = (pw * lax.rsqrt(ms + EPS) * g_ref[...] * (z * jax.nn.sigmoid(z))).astype(o_ref.dtype)


def _conv_module(proj, w_dw, b_dw, ln_g, ln_b, w_pw_bf, b_pw, g, *, batch, seq, width, col0, tt):
    m = proj.shape[0]
    tiles = seq // tt
    w_dw_p = jnp.zeros((HALO, width), F32).at[:CONV_K].set(w_dw)

    def spec(col):
        return pl.BlockSpec((tt, width), lambda b, t: (b * tiles + t, col))

    def vec():
        return pl.BlockSpec((1, width), lambda b, t: (0, 0))

    return pl.pallas_call(
        functools.partial(_conv_kernel, strip_rows=64, strip_cols=256),
        out_shape=jax.ShapeDtypeStruct((m, width), BF16),
        grid=(batch, tiles),
        in_specs=[spec(col0), spec(col0 + 1), spec(col0 + 2),
                  pl.BlockSpec((HALO, width), lambda b, t: (0, 0)),
                  vec(), vec(), vec(),
                  pl.BlockSpec((width, width), lambda b, t: (0, 0)),
                  vec(), vec()],
        out_specs=pl.BlockSpec((tt, width), lambda b, t: (b * tiles + t, 0)),
        scratch_shapes=[pltpu.VMEM((tt + HALO, width), F32), pltpu.VMEM((tt, width), F32)],
        compiler_params=_cparams(2),
        name="conv_module",
    )(proj, proj, proj, w_dw_p, b_dw.reshape(1, width), ln_g.reshape(1, width),
      ln_b.reshape(1, width), w_pw_bf, b_pw.reshape(1, width), g.reshape(1, width))


def _out_kernel(x_ref, ya_ref, yc_ref, wo_ref, pg_ref, wg_ref, bg_ref, p_ref, wp_ref, fg_ref,
                o_ref, hn_ref, ss_ref, *, nj, tn, apply_final):
    j = pl.program_id(1)
    wa = ya_ref.shape[1]

    @pl.when(j < nj)
    def _():
        cols = pl.ds(pl.multiple_of(j * tn, tn), tn)
        acc = jnp.dot(ya_ref[...], wo_ref[0:wa, :], preferred_element_type=F32)
        acc += jnp.dot(yc_ref[...], wo_ref[wa:, :], preferred_element_type=F32)
        o_ref[:, cols] = x_ref[...] + acc

    @pl.when(j == nj)
    def _():
        h = o_ref[...]
        ms = jnp.mean(h * h, axis=-1, keepdims=True)
        hn_ref[...] = (h * lax.rsqrt(ms + EPS) * pg_ref[...]).astype(BF16)
        ss_ref[...] = jnp.zeros_like(ss_ref)

    @pl.when(j >= nj)
    def _():
        cols = pl.ds(pl.multiple_of((j - nj) * tn, tn), tn)
        gate = jax.nn.sigmoid(
            jnp.dot(hn_ref[...], wg_ref[...], preferred_element_type=F32) + bg_ref[...])
        pe = jnp.dot(p_ref[...].astype(BF16), wp_ref[...], preferred_element_type=F32)
        h2 = o_ref[:, cols] + gate * pe
        o_ref[:, cols] = h2
        ss_ref[...] += jnp.sum(h2 * h2, axis=-1, keepdims=True)

    if apply_final:
        @pl.when(j == 2 * nj - 1)
        def _():
            h2 = o_ref[...]
            ms = ss_ref[...] * (1.0 / h2.shape[1])
            o_ref[...] = h2 * lax.rsqrt(ms + EPS) * fg_ref[...]


def _out_block(x2, y_a, y_c, w_out_bf, ple_g, w_gate_bf, b_gate, p2, w_ple_bf, final_g,
               *, tm, tn, apply_final):
    m, d = x2.shape
    wa = y_a.shape[1]
    wc = y_c.shape[1]
    pd = p2.shape[1]
    nj = d // tn

    def first(i, j):
        return jnp.minimum(j, nj - 1)

    def second(i, j):
        return jnp.maximum(j - nj, 0)

    return pl.pallas_call(
        functools.partial(_out_kernel, nj=nj, tn=tn, apply_final=apply_final),
        out_shape=jax.ShapeDtypeStruct((m, d), F32),
        grid=(m // tm, 2 * nj),
        in_specs=[
            pl.BlockSpec((tm, tn), lambda i, j: (i, first(i, j))),
            pl.BlockSpec((tm, wa), lambda i, j: (i, 0)),
            pl.BlockSpec((tm, wc), lambda i, j: (i, 0)),
            pl.BlockSpec((d, tn), lambda i, j: (0, first(i, j))),
            pl.BlockSpec((1, d), lambda i, j: (0, 0)),
            pl.BlockSpec((d, tn), lambda i, j: (0, second(i, j))),
            pl.BlockSpec((1, tn), lambda i, j: (0, second(i, j))),
            pl.BlockSpec((tm, pd), lambda i, j: (i, 0)),
            pl.BlockSpec((pd, tn), lambda i, j: (0, second(i, j))),
            pl.BlockSpec((1, d), lambda i, j: (0, 0)),
        ],
        out_specs=pl.BlockSpec((tm, d), lambda i, j: (i, 0)),
        scratch_shapes=[pltpu.VMEM((tm, d), BF16), pltpu.VMEM((tm, 1), F32)],
        compiler_params=_cparams(2),
        name="out_block",
    )(x2, y_a, y_c, w_out_bf, ple_g.reshape(1, d), w_gate_bf, b_gate.reshape(1, d), p2, w_ple_bf,
      final_g.reshape(1, d))


def kernel(x, p, norm_in_g, w_in, rel_table, w_dw, b_dw, conv_ln_g, conv_ln_b, w_pw, b_pw,
           attn_out_g, conv_out_g, w_out, ple_norm_g, w_ple_gate, b_ple_gate, w_ple, final_g):
    batch, seq, d = x.shape
    depth = w_in.shape[0]
    conv_width = w_pw.shape[1]
    att_width = d - conv_width
    m = batch * seq
    assert seq % Q_TILE == 0 and Q_TILE % CHUNK == 0
    assert (KV_BLOCKS - 1) * Q_TILE == LEFT_CHUNKS * CHUNK
    assert att_width == conv_width

    tm = min(512, m)
    h = x.reshape(m, d)
    for i in range(depth):
        last = i == depth - 1
        proj = _in_projection(h, norm_in_g[i], w_in[i].astype(BF16), tm=tm, tn=min(1024, d))
        y_a = _attention(proj, _band_bias(rel_table[i]), attn_out_g[i],
                         batch=batch, seq=seq, width=att_width, n_heads=N_HEADS)
        y_c = _conv_module(proj, w_dw[i], b_dw[i], conv_ln_g[i], conv_ln_b[i],
                           w_pw[i].astype(BF16), b_pw[i], conv_out_g[i],
                           batch=batch, seq=seq, width=conv_width, col0=4, tt=min(256, seq))
        h = _out_block(h, y_a, y_c, w_out[i].astype(BF16), ple_norm_g[i],
                       w_ple_gate[i].astype(BF16), b_ple_gate[i], p[i].reshape(m, -1),
                       w_ple[i].astype(BF16), final_g, tm=tm, tn=min(512, d), apply_final=last)
    if depth == 0:
        raise ValueError("depth must be positive")
    return h.reshape(batch, seq, d)
```

```python
import functools

import numpy as np
import jax
import jax.numpy as jnp
from jax import lax
from jax.experimental import pallas as pl
from jax.experimental.pallas import tpu as pltpu

CHUNK = 64
LEFT_CHUNKS = 8
N_HEADS = 16
MAX_REL = 256
CONV_K = 31
EPS = 1e-6
NEG_INF = -1e30

LANES = 128
Q_TILE = 256
KV_BLOCKS = 3
BAND_KEYS = KV_BLOCKS * Q_TILE
BIAS_ROW = 1024
HALO = 32
ROW_CHUNK = 32
VMEM_LIMIT = 56 * 1024 * 1024

F32 = jnp.float32
BF16 = jnp.bfloat16


def _cparams(n_axes):
    return pltpu.CompilerParams(
        dimension_semantics=("arbitrary",) * n_axes, vmem_limit_bytes=VMEM_LIMIT)


def _row_loop(n_rows, body):
    def step(r, carry):
        body(pl.ds(pl.multiple_of(r * ROW_CHUNK, ROW_CHUNK), ROW_CHUNK))
        return carry
    lax.fori_loop(0, n_rows // ROW_CHUNK, step, 0)


def _silu(z):
    return z * jax.nn.sigmoid(z)


def _proj_kernel(x_ref, g_ref, w_ref, o_ref, xn_ref):
    @pl.when(pl.program_id(1) == 0)
    def _():
        def norm_rows(rows):
            x = x_ref[rows, :]
            ms = jnp.mean(x * x, axis=-1, keepdims=True)
            xn_ref[rows, :] = (x * lax.rsqrt(ms + EPS) * g_ref[...]).astype(BF16)
        _row_loop(x_ref.shape[0], norm_rows)

    o_ref[...] = jnp.dot(xn_ref[...], w_ref[...], preferred_element_type=F32).astype(o_ref.dtype)


def _in_projection(x2, g, w_bf, *, tm, tn):
    m, d = x2.shape
    n = w_bf.shape[1]
    return pl.pallas_call(
        _proj_kernel,
        out_shape=jax.ShapeDtypeStruct((m, n), BF16),
        grid=(m // tm, n // tn),
        in_specs=[
            pl.BlockSpec((tm, d), lambda i, j: (i, 0)),
            pl.BlockSpec((1, d), lambda i, j: (0, 0)),
            pl.BlockSpec((d, tn), lambda i, j: (0, j)),
        ],
        out_specs=pl.BlockSpec((tm, tn), lambda i, j: (i, j)),
        scratch_shapes=[pltpu.VMEM((tm, d), BF16)],
        compiler_params=_cparams(2),
        name="in_projection",
    )(x2, g.reshape(1, d), w_bf)


def _attn_kernel(q_ref, k0_ref, k1_ref, k2_ref, v0_ref, v1_ref, v2_ref, z_ref, brow_ref, g_ref,
                 o_ref, y_ref, bias_ref, *, n_heads, head_dim):
    t = pl.program_id(1)
    k_refs = (k0_ref, k1_ref, k2_ref)
    v_refs = (v0_ref, v1_ref, v2_ref)

    @pl.when((pl.program_id(0) == 0) & (t == 0))
    def _():
        qi = lax.broadcasted_iota(jnp.int32, (Q_TILE, BAND_KEYS), 0) // CHUNK
        kj = lax.broadcasted_iota(jnp.int32, (Q_TILE, BAND_KEYS), 1) // CHUNK
        in_band = (kj >= qi) & (kj <= qi + LEFT_CHUNKS)
        for h in range(n_heads):
            row = jnp.broadcast_to(brow_ref[h:h + 1, :], (Q_TILE, BIAS_ROW))
            skew = pltpu.roll(row, 0, 1, stride=1, stride_axis=0)
            bias_ref[h] = jnp.where(in_band, skew[:, :BAND_KEYS], NEG_INF)

    def head(h, carry):
        cols = pl.ds(pl.multiple_of(h * head_dim, head_dim), head_dim)
        q = q_ref[:, cols]
        s = []
        for j in range(KV_BLOCKS):
            sj = lax.dot_general(q, k_refs[j][:, cols], (((1,), (1,)), ((), ())),
                                 preferred_element_type=F32)
            sj = sj + bias_ref[h, :, j * Q_TILE:(j + 1) * Q_TILE]
            if j < KV_BLOCKS - 1:
                sj = jnp.where(t + (j - (KV_BLOCKS - 1)) >= 0, sj, NEG_INF)
            s.append(sj)
        m = jnp.max(jnp.maximum(jnp.maximum(s[0], s[1]), s[2]), axis=-1, keepdims=True)
        p = [jnp.exp(sj - m) for sj in s]
        l = jnp.sum(p[0] + p[1] + p[2], axis=-1, keepdims=True)
        o = jnp.dot(p[0].astype(BF16), v_refs[0][:, cols], preferred_element_type=F32)
        o += jnp.dot(p[1].astype(BF16), v_refs[1][:, cols], preferred_element_type=F32)
        o += jnp.dot(p[2].astype(BF16), v_refs[2][:, cols], preferred_element_type=F32)
        y_ref[:, cols] = o / l
        return carry

    lax.fori_loop(0, n_heads, head, 0, unroll=2)

    def finish_rows(rows):
        y = y_ref[rows, :]
        ms = jnp.mean(y * y, axis=-1, keepdims=True)
        z = z_ref[rows, :].astype(F32)
        o_ref[rows, :] = (y * lax.rsqrt(ms + EPS) * g_ref[...] * _silu(z)).astype(o_ref.dtype)
    _row_loop(Q_TILE, finish_rows)


def _bias_rows(table):
    e = np.arange(BIAS_ROW)
    e = np.where(e >= BAND_KEYS, e - BIAS_ROW, e)
    idx = np.clip((KV_BLOCKS - 1) * Q_TILE - e, -MAX_REL, MAX_REL) + MAX_REL
    return table[:, idx].astype(F32)


def _attention(proj, brow, g, *, batch, seq, width, n_heads):
    m = proj.shape[0]
    tiles = seq // Q_TILE
    head_dim = width // n_heads

    def spec(off, col):
        return pl.BlockSpec((Q_TILE, width),
                            lambda b, t: (b * tiles + jnp.maximum(t + off, 0), col))

    in_specs = [spec(0, 0),
                spec(-2, 1), spec(-1, 1), spec(0, 1),
                spec(-2, 2), spec(-1, 2), spec(0, 2),
                spec(0, 3),
                pl.BlockSpec(brow.shape, lambda b, t: (0, 0)),
                pl.BlockSpec((1, width), lambda b, t: (0, 0))]
    return pl.pallas_call(
        functools.partial(_attn_kernel, n_heads=n_heads, head_dim=head_dim),
        out_shape=jax.ShapeDtypeStruct((m, width), BF16),
        grid=(batch, tiles),
        in_specs=in_specs,
        out_specs=pl.BlockSpec((Q_TILE, width), lambda b, t: (b * tiles + t, 0)),
        scratch_shapes=[pltpu.VMEM((Q_TILE, width), F32),
                        pltpu.VMEM((n_heads, Q_TILE, BAND_KEYS), F32)],
        compiler_params=_cparams(2),
        name="band_attention",
    )(*([proj] * 8), brow, g.reshape(1, width))


def _conv_kernel(a_ref, gl_ref, z_ref, wdw_ref, bdw_ref, lng_ref, lnb_ref, wpw_ref, bpw_ref, g_ref,
                 o_ref, u2_ref, c_ref, yb_ref, pw_ref, *, strip_rows):
    tt, width = a_ref.shape
    n_slabs = width // LANES
    t = pl.program_id(1)

    @pl.when(t == 0)
    def _():
        u2_ref[...] = jnp.zeros_like(u2_ref)

    @pl.when(t > 0)
    def _():
        u2_ref[:, 0:2 * HALO, :] = u2_ref[:, 2 * tt:2 * (tt + HALO), :]

    def glu_rows(rows):
        u = a_ref[rows, :].astype(F32) * jax.nn.sigmoid(gl_ref[rows, :].astype(F32))
        for c in range(n_slabs):
            u2_ref[c, pl.ds(2 * (HALO + rows.start), ROW_CHUNK, stride=2), :] = (
                u[:, c * LANES:(c + 1) * LANES])
    _row_loop(tt, glu_rows)

    def conv_slab(c, carry):
        cols = pl.ds(pl.multiple_of(c * LANES, LANES), LANES)
        for r0 in range(0, tt, strip_rows):
            acc = jnp.zeros((strip_rows, LANES), F32)
            for k in range(CONV_K):
                first = r0 + HALO - (CONV_K - 1) + k
                acc += u2_ref[c, pl.ds(2 * first, strip_rows, stride=2), :] * wdw_ref[k:k + 1, cols]
            c_ref[r0:r0 + strip_rows, cols] = acc + bdw_ref[:, cols]
        return carry
    lax.fori_loop(0, n_slabs, conv_slab, 0)

    def norm_rows(rows):
        c = c_ref[rows, :]
        mu = jnp.mean(c, axis=-1, keepdims=True)
        cc = c - mu
        var = jnp.mean(cc * cc, axis=-1, keepdims=True)
        y = cc * lax.rsqrt(var + EPS) * lng_ref[...] + lnb_ref[...]
        yb_ref[rows, :] = _silu(y).astype(BF16)
    _row_loop(tt, norm_rows)

    pw_ref[...] = jnp.dot(yb_ref[...], wpw_ref[...], preferred_element_type=F32)

    def finish_rows(rows):
        pw = pw_ref[rows, :] + bpw_ref[...]
        ms = jnp.mean(pw * pw, axis=-1, keepdims=True)
        z = z_ref[rows, :].astype(F32)
        o_ref[rows, :] = (pw * lax.rsqrt(ms + EPS) * g_ref[...] * _silu(z)).astype(o_ref.dtype)
    _row_loop(tt, finish_rows)


def _conv_module(proj, w_dw, b_dw, ln_g, ln_b, w_pw_bf, b_pw, g, *, batch, seq, width, col0, tt):
    m = proj.shape[0]
    tiles = seq // tt
    w_dw_p = jnp.zeros((HALO, width), F32).at[:CONV_K].set(w_dw)

    def spec(col):
        return pl.BlockSpec((tt, width), lambda b, t: (b * tiles + t, col))

    def vec():
        return pl.BlockSpec((1, width), lambda b, t: (0, 0))

    return pl.pallas_call(
        functools.partial(_conv_kernel, strip_rows=128),
        out_shape=jax.ShapeDtypeStruct((m, width), BF16),
        grid=(batch, tiles),
        in_specs=[spec(col0), spec(col0 + 1), spec(col0 + 2),
                  pl.BlockSpec((HALO, width), lambda b, t: (0, 0)),
                  vec(), vec(), vec(),
                  pl.BlockSpec((width, width), lambda b, t: (0, 0)),
                  vec(), vec()],
        out_specs=pl.BlockSpec((tt, width), lambda b, t: (b * tiles + t, 0)),
        scratch_shapes=[pltpu.VMEM((width // LANES, 2 * (tt + HALO), LANES), F32),
                        pltpu.VMEM((tt, width), F32),
                        pltpu.VMEM((tt, width), BF16),
                        pltpu.VMEM((tt, width), F32)],
        compiler_params=_cparams(2),
        name="conv_module",
    )(proj, proj, proj, w_dw_p, b_dw.reshape(1, width), ln_g.reshape(1, width),
      ln_b.reshape(1, width), w_pw_bf, b_pw.reshape(1, width), g.reshape(1, width))


def _out_kernel(x_ref, ya_ref, yc_ref, wo_ref, pg_ref, wg_ref, bg_ref, p_ref, wp_ref, fg_ref,
                o_ref, hn_ref, ss_ref, *, nj, tn, apply_final):
    j = pl.program_id(1)
    wa = ya_ref.shape[1]

    @pl.when(j < nj)
    def _():
        cols = pl.ds(pl.multiple_of(j * tn, tn), tn)
        acc = jnp.dot(ya_ref[...], wo_ref[0:wa, :], preferred_element_type=F32)
        acc += jnp.dot(yc_ref[...], wo_ref[wa:, :], preferred_element_type=F32)
        o_ref[:, cols] = x_ref[...] + acc

    @pl.when(j == nj)
    def _():
        h = o_ref[...]
        ms = jnp.mean(h * h, axis=-1, keepdims=True)
        hn_ref[...] = (h * lax.rsqrt(ms + EPS) * pg_ref[...]).astype(BF16)
        ss_ref[...] = jnp.zeros_like(ss_ref)

    @pl.when(j >= nj)
    def _():
        cols = pl.ds(pl.multiple_of((j - nj) * tn, tn), tn)
        gate = jax.nn.sigmoid(
            jnp.dot(hn_ref[...], wg_ref[...], preferred_element_type=F32) + bg_ref[...])
        pe = jnp.dot(p_ref[...].astype(BF16), wp_ref[...], preferred_element_type=F32)
        h2 = o_ref[:, cols] + gate * pe
        o_ref[:, cols] = h2
        ss_ref[...] += jnp.sum(h2 * h2, axis=-1, keepdims=True)

    if apply_final:
        @pl.when(j == 2 * nj - 1)
        def _():
            h2 = o_ref[...]
            ms = ss_ref[...] * (1.0 / h2.shape[1])
            o_ref[...] = h2 * lax.rsqrt(ms + EPS) * fg_ref[...]


def _out_block(x2, y_a, y_c, w_out_bf, ple_g, w_gate_bf, b_gate, p2, w_ple_bf, final_g,
               *, tm, tn, apply_final):
    m, d = x2.shape
    wa = y_a.shape[1]
    wc = y_c.shape[1]
    pd = p2.shape[1]
    nj = d // tn

    def first(i, j):
        return jnp.minimum(j, nj - 1)

    def second(i, j):
        return jnp.maximum(j - nj, 0)

    return pl.pallas_call(
        functools.partial(_out_kernel, nj=nj, tn=tn, apply_final=apply_final),
        out_shape=jax.ShapeDtypeStruct((m, d), F32),
        grid=(m // tm, 2 * nj),
        in_specs=[
            pl.BlockSpec((tm, tn), lambda i, j: (i, first(i, j))),
            pl.BlockSpec((tm, wa), lambda i, j: (i, 0)),
            pl.BlockSpec((tm, wc), lambda i, j: (i, 0)),
            pl.BlockSpec((d, tn), lambda i, j: (0, first(i, j))),
            pl.BlockSpec((1, d), lambda i, j: (0, 0)),
            pl.BlockSpec((d, tn), lambda i, j: (0, second(i, j))),
            pl.BlockSpec((1, tn), lambda i, j: (0, second(i, j))),
            pl.BlockSpec((tm, pd), lambda i, j: (i, 0)),
            pl.BlockSpec((pd, tn), lambda i, j: (0, second(i, j))),
            pl.BlockSpec((1, d), lambda i, j: (0, 0)),
        ],
        out_specs=pl.BlockSpec((tm, d), lambda i, j: (i, 0)),
        scratch_shapes=[pltpu.VMEM((tm, d), BF16), pltpu.VMEM((tm, 1), F32)],
        compiler_params=_cparams(2),
        name="out_block",
    )(x2, y_a, y_c, w_out_bf, ple_g.reshape(1, d), w_gate_bf, b_gate.reshape(1, d), p2, w_ple_bf,
      final_g.reshape(1, d))


def kernel(x, p, norm_in_g, w_in, rel_table, w_dw, b_dw, conv_ln_g, conv_ln_b, w_pw, b_pw,
           attn_out_g, conv_out_g, w_out, ple_norm_g, w_ple_gate, b_ple_gate, w_ple, final_g):
    batch, seq, d = x.shape
    depth = w_in.shape[0]
    conv_width = w_pw.shape[1]
    att_width = d - conv_width
    m = batch * seq
    assert depth >= 1
    assert seq % Q_TILE == 0 and Q_TILE % CHUNK == 0
    assert (KV_BLOCKS - 1) * Q_TILE == LEFT_CHUNKS * CHUNK
    assert att_width == conv_width
    head_dim = att_width // N_HEADS
    col_scale = jnp.where(jnp.arange(w_in.shape[2]) < att_width, head_dim ** -0.5, 1.0).astype(F32)

    tm = min(512, m)
    h = x.reshape(m, d)
    for i in range(depth):
        last = i == depth - 1
        proj = _in_projection(h, norm_in_g[i], (w_in[i] * col_scale).astype(BF16),
                              tm=tm, tn=min(1024, d))
        y_a = _attention(proj, _bias_rows(rel_table[i]), attn_out_g[i],
                         batch=batch, seq=seq, width=att_width, n_heads=N_HEADS)
        y_c = _conv_module(proj, w_dw[i], b_dw[i], conv_ln_g[i], conv_ln_b[i],
                           w_pw[i].astype(BF16), b_pw[i], conv_out_g[i],
                           batch=batch, seq=seq, width=conv_width, col0=4, tt=min(256, seq))
        h = _out_block(h, y_a, y_c, w_out[i].astype(BF16), ple_norm_g[i],
                       w_ple_gate[i].astype(BF16), b_ple_gate[i], p[i].reshape(m, -1),
                       w_ple[i].astype(BF16), final_g, tm=tm, tn=min(512, d), apply_final=last)
    return h.reshape(batch, seq, d)
```

```python
import functools

import numpy as np
import jax
import jax.numpy as jnp
from jax import lax
from jax.experimental import pallas as pl
from jax.experimental.pallas import tpu as pltpu

CHUNK = 64
LEFT_CHUNKS = 8
N_HEADS = 16
MAX_REL = 256
CONV_K = 31
EPS = 1e-6
NEG_INF = -1e30
LOG2_E = 1.4426950408889634

LANES = 128
Q_TILE = 256
KV_BLOCKS = 3
BAND_KEYS = KV_BLOCKS * Q_TILE
BIAS_ROW = 1024
HALO = 32
ROW_CHUNK = 32
VMEM_LIMIT = 56 * 1024 * 1024

F32 = jnp.float32
BF16 = jnp.bfloat16


def _cparams(n_axes):
    return pltpu.CompilerParams(
        dimension_semantics=("arbitrary",) * n_axes, vmem_limit_bytes=VMEM_LIMIT)


def _row_loop(n_rows, body, unroll=1):
    def step(r, carry):
        body(pl.ds(pl.multiple_of(r * ROW_CHUNK, ROW_CHUNK), ROW_CHUNK))
        return carry
    lax.fori_loop(0, n_rows // ROW_CHUNK, step, 0, unroll=unroll)


def _silu(z):
    return z * jax.nn.sigmoid(z)


def _proj_kernel(x_ref, g_ref, w_ref, o_ref, xn_ref):
    @pl.when(pl.program_id(1) == 0)
    def _():
        def norm_rows(rows):
            x = x_ref[rows, :]
            ms = jnp.mean(x * x, axis=-1, keepdims=True)
            xn_ref[rows, :] = (x * lax.rsqrt(ms + EPS) * g_ref[...]).astype(BF16)
        _row_loop(x_ref.shape[0], norm_rows)

    o_ref[...] = jnp.dot(xn_ref[...], w_ref[...], preferred_element_type=F32).astype(o_ref.dtype)


def _in_projection(x2, g, w_bf, *, tm, tn):
    m, d = x2.shape
    n = w_bf.shape[1]
    return pl.pallas_call(
        _proj_kernel,
        out_shape=jax.ShapeDtypeStruct((m, n), BF16),
        grid=(m // tm, n // tn),
        in_specs=[
            pl.BlockSpec((tm, d), lambda i, j: (i, 0)),
            pl.BlockSpec((1, d), lambda i, j: (0, 0)),
            pl.BlockSpec((d, tn), lambda i, j: (0, j)),
        ],
        out_specs=pl.BlockSpec((tm, tn), lambda i, j: (i, j)),
        scratch_shapes=[pltpu.VMEM((tm, d), BF16)],
        compiler_params=_cparams(2),
        name="in_projection",
    )(x2, g.reshape(1, d), w_bf)


def _attn_kernel(q_ref, k0_ref, k1_ref, k2_ref, v0_ref, v1_ref, v2_ref, z_ref, brow_ref, g_ref,
                 o_ref, y_ref, bias_ref, vext_ref, *, n_heads, head_dim):
    t = pl.program_id(1)
    k_refs = (k0_ref, k1_ref, k2_ref)
    v_refs = (v0_ref, v1_ref, v2_ref)

    @pl.when((pl.program_id(0) == 0) & (t == 0))
    def _():
        qi = lax.broadcasted_iota(jnp.int32, (Q_TILE, BAND_KEYS), 0) // CHUNK
        kj = lax.broadcasted_iota(jnp.int32, (Q_TILE, BAND_KEYS), 1) // CHUNK
        in_band = (kj >= qi) & (kj <= qi + LEFT_CHUNKS)
        for h in range(n_heads):
            row = jnp.broadcast_to(brow_ref[h:h + 1, :], (Q_TILE, BIAS_ROW))
            skew = pltpu.roll(row, 0, 1, stride=1, stride_axis=0)
            bias_ref[h, :, 0:BAND_KEYS] = jnp.where(in_band, skew[:, :BAND_KEYS], NEG_INF)
            bias_ref[h, :, BAND_KEYS:] = jnp.full((Q_TILE, Q_TILE), NEG_INF, F32)
        vext_ref[:, :, head_dim:] = jnp.ones((n_heads, BAND_KEYS, head_dim), BF16)

    for h in range(n_heads):
        cols = slice(h * head_dim, (h + 1) * head_dim)
        q = q_ref[:, cols]
        s = []
        for j in range(KV_BLOCKS):
            vext_ref[h, j * Q_TILE:(j + 1) * Q_TILE, 0:head_dim] = v_refs[j][:, cols]
            sj = lax.dot_general(q, k_refs[j][:, cols], (((1,), (1,)), ((), ())),
                                 preferred_element_type=F32)
            if j < KV_BLOCKS - 1:
                valid = t + (j - (KV_BLOCKS - 1)) >= 0
                off = pl.multiple_of(jnp.where(valid, j * Q_TILE, BAND_KEYS), Q_TILE)
            else:
                off = j * Q_TILE
            s.append(sj + bias_ref[h, :, pl.ds(off, Q_TILE)])
        m = jnp.max(jnp.maximum(jnp.maximum(s[0], s[1]), s[2]), axis=-1, keepdims=True)
        oe = None
        for j in range(KV_BLOCKS):
            pj = jnp.exp2(s[j] - m).astype(BF16)
            part = jnp.dot(pj, vext_ref[h, j * Q_TILE:(j + 1) * Q_TILE, :],
                           preferred_element_type=F32)
            oe = part if oe is None else oe + part
        y_ref[:, cols] = oe[:, 0:head_dim] / oe[:, head_dim:]

    def finish_rows(rows):
        y = y_ref[rows, :]
        ms = jnp.mean(y * y, axis=-1, keepdims=True)
        z = z_ref[rows, :].astype(F32)
        o_ref[rows, :] = (y * lax.rsqrt(ms + EPS) * g_ref[...] * _silu(z)).astype(o_ref.dtype)
    _row_loop(Q_TILE, finish_rows)


def _bias_rows(table):
    e = np.arange(BIAS_ROW)
    e = np.where(e >= BAND_KEYS, e - BIAS_ROW, e)
    idx = np.clip((KV_BLOCKS - 1) * Q_TILE - e, -MAX_REL, MAX_REL) + MAX_REL
    return table[:, idx].astype(F32) * LOG2_E


def _attention(proj, brow, g, *, batch, seq, width, n_heads):
    m = proj.shape[0]
    tiles = seq // Q_TILE
    head_dim = width // n_heads

    def spec(off, col):
        return pl.BlockSpec((Q_TILE, width),
                            lambda b, t: (b * tiles + jnp.maximum(t + off, 0), col))

    in_specs = [spec(0, 0),
                spec(-2, 1), spec(-1, 1), spec(0, 1),
                spec(-2, 2), spec(-1, 2), spec(0, 2),
                spec(0, 3),
                pl.BlockSpec(brow.shape, lambda b, t: (0, 0)),
                pl.BlockSpec((1, width), lambda b, t: (0, 0))]
    return pl.pallas_call(
        functools.partial(_attn_kernel, n_heads=n_heads, head_dim=head_dim),
        out_shape=jax.ShapeDtypeStruct((m, width), BF16),
        grid=(batch, tiles),
        in_specs=in_specs,
        out_specs=pl.BlockSpec((Q_TILE, width), lambda b, t: (b * tiles + t, 0)),
        scratch_shapes=[pltpu.VMEM((Q_TILE, width), F32),
                        pltpu.VMEM((n_heads, Q_TILE, BAND_KEYS + Q_TILE), F32),
                        pltpu.VMEM((n_heads, BAND_KEYS, 2 * head_dim), BF16)],
        compiler_params=_cparams(2),
        name="band_attention",
    )(*([proj] * 8), brow, g.reshape(1, width))


def _conv_kernel(a_ref, gl_ref, z_ref, wdw_ref, bdw_ref, lng_ref, lnb_ref, wpw_ref, bpw_ref, g_ref,
                 o_ref, u2_ref, c_ref, yb_ref, pw_ref, *, strip_rows):
    tt, width = a_ref.shape
    n_slabs = width // LANES
    t = pl.program_id(1)

    @pl.when(t == 0)
    def _():
        u2_ref[...] = jnp.zeros_like(u2_ref)

    @pl.when(t > 0)
    def _():
        u2_ref[:, 0:2 * HALO, :] = u2_ref[:, 2 * tt:2 * (tt + HALO), :]

    def glu_rows(rows):
        u = a_ref[rows, :].astype(F32) * jax.nn.sigmoid(gl_ref[rows, :].astype(F32))
        for c in range(n_slabs):
            u2_ref[c, pl.ds(2 * (HALO + rows.start), ROW_CHUNK, stride=2), :] = (
                u[:, c * LANES:(c + 1) * LANES])
    _row_loop(tt, glu_rows)

    def conv_slab(c, carry):
        cols = pl.ds(pl.multiple_of(c * LANES, LANES), LANES)
        for r0 in range(0, tt, strip_rows):
            acc = jnp.zeros((strip_rows, LANES), F32)
            for k in range(CONV_K):
                first = r0 + HALO - (CONV_K - 1) + k
                acc += u2_ref[c, pl.ds(2 * first, strip_rows, stride=2), :] * wdw_ref[k:k + 1, cols]
            c_ref[r0:r0 + strip_rows, cols] = acc + bdw_ref[:, cols]
        return carry
    lax.fori_loop(0, n_slabs, conv_slab, 0)

    def norm_rows(rows):
        c = c_ref[rows, :]
        mu = jnp.mean(c, axis=-1, keepdims=True)
        cc = c - mu
        var = jnp.mean(cc * cc, axis=-1, keepdims=True)
        y = cc * lax.rsqrt(var + EPS) * lng_ref[...] + lnb_ref[...]
        yb_ref[rows, :] = _silu(y).astype(BF16)
    _row_loop(tt, norm_rows, unroll=2)

    pw_ref[...] = jnp.dot(yb_ref[...], wpw_ref[...], preferred_element_type=F32)

    def finish_rows(rows):
        pw = pw_ref[rows, :] + bpw_ref[...]
        ms = jnp.mean(pw * pw, axis=-1, keepdims=True)
        z = z_ref[rows, :].astype(F32)
        o_ref[rows, :] = (pw * lax.rsqrt(ms + EPS) * g_ref[...] * _silu(z)).astype(o_ref.dtype)
    _row_loop(tt, finish_rows)


def _conv_module(proj, w_dw, b_dw, ln_g, ln_b, w_pw_bf, b_pw, g, *, batch, seq, width, col0, tt):
    m = proj.shape[0]
    tiles = seq // tt
    w_dw_p = jnp.zeros((HALO, width), F32).at[:CONV_K].set(w_dw)

    def spec(col):
        return pl.BlockSpec((tt, width), lambda b, t: (b * tiles + t, col))

    def vec():
        return pl.BlockSpec((1, width), lambda b, t: (0, 0))

    return pl.pallas_call(
        functools.partial(_conv_kernel, strip_rows=128),
        out_shape=jax.ShapeDtypeStruct((m, width), BF16),
        grid=(batch, tiles),
        in_specs=[spec(col0), spec(col0 + 1), spec(col0 + 2),
                  pl.BlockSpec((HALO, width), lambda b, t: (0, 0)),
                  vec(), vec(), vec(),
                  pl.BlockSpec((width, width), lambda b, t: (0, 0)),
                  vec(), vec()],
        out_specs=pl.BlockSpec((tt, width), lambda b, t: (b * tiles + t, 0)),
        scratch_shapes=[pltpu.VMEM((width // LANES, 2 * (tt + HALO), LANES), F32),
                        pltpu.VMEM((tt, width), F32),
                        pltpu.VMEM((tt, width), BF16),
                        pltpu.VMEM((tt, width), F32)],
        compiler_params=_cparams(2),
        name="conv_module",
    )(proj, proj, proj, w_dw_p, b_dw.reshape(1, width), ln_g.reshape(1, width),
      ln_b.reshape(1, width), w_pw_bf, b_pw.reshape(1, width), g.reshape(1, width))


def _out_kernel(x_ref, ya_ref, yc_ref, w_ref, pg_ref, bg_ref, p_ref, wp_ref, fg_ref,
                o_ref, hn_ref, ssh_ref, sso_ref, *, nj, tn, apply_final):
    j = pl.program_id(1)
    wa = ya_ref.shape[1]
    d = o_ref.shape[1]

    @pl.when(j == 0)
    def _():
        ssh_ref[...] = jnp.zeros_like(ssh_ref)
        sso_ref[...] = jnp.zeros_like(sso_ref)

    @pl.when(j < nj)
    def _():
        cols = pl.ds(pl.multiple_of(j * tn, tn), tn)
        acc = jnp.dot(ya_ref[...], w_ref[0:wa, :], preferred_element_type=F32)
        acc += jnp.dot(yc_ref[...], w_ref[wa:, :], preferred_element_type=F32)
        h = x_ref[...] + acc
        o_ref[:, cols] = h
        hn_ref[:, cols] = (h * pg_ref[:, cols]).astype(BF16)
        ssh_ref[...] += jnp.sum(h * h, axis=-1, keepdims=True)

    @pl.when(j >= nj)
    def _():
        cols = pl.ds(pl.multiple_of((j - nj) * tn, tn), tn)
        rinv = lax.rsqrt(ssh_ref[...] * (1.0 / d) + EPS)
        gate = jax.nn.sigmoid(
            jnp.dot(hn_ref[...], w_ref[...], preferred_element_type=F32) * rinv + bg_ref[...])
        pe = jnp.dot(p_ref[...].astype(BF16), wp_ref[...], preferred_element_type=F32)
        h2 = o_ref[:, cols] + gate * pe
        o_ref[:, cols] = h2
        sso_ref[...] += jnp.sum(h2 * h2, axis=-1, keepdims=True)

    if apply_final:
        @pl.when(j == 2 * nj - 1)
        def _():
            def final_rows(rows):
                rinv = lax.rsqrt(sso_ref[rows, :] * (1.0 / d) + EPS)
                o_ref[rows, :] = o_ref[rows, :] * rinv * fg_ref[...]
            _row_loop(o_ref.shape[0], final_rows)


def _out_block(x2, y_a, y_c, w_cat_bf, ple_g, b_gate, p2, w_ple_bf, final_g, *, tm, tn, apply_final):
    m, d = x2.shape
    wa = y_a.shape[1]
    wc = y_c.shape[1]
    pd = p2.shape[1]
    nj = d // tn

    def first(i, j):
        return jnp.minimum(j, nj - 1)

    def second(i, j):
        return jnp.maximum(j - nj, 0)

    return pl.pallas_call(
        functools.partial(_out_kernel, nj=nj, tn=tn, apply_final=apply_final),
        out_shape=jax.ShapeDtypeStruct((m, d), F32),
        grid=(m // tm, 2 * nj),
        in_specs=[
            pl.BlockSpec((tm, tn), lambda i, j: (i, first(i, j))),
            pl.BlockSpec((tm, wa), lambda i, j: (i, 0)),
            pl.BlockSpec((tm, wc), lambda i, j: (i, 0)),
            pl.BlockSpec((d, tn), lambda i, j: (0, j)),
            pl.BlockSpec((1, d), lambda i, j: (0, 0)),
            pl.BlockSpec((1, tn), lambda i, j: (0, second(i, j))),
            pl.BlockSpec((tm, pd), lambda i, j: (i, 0)),
            pl.BlockSpec((pd, tn), lambda i, j: (0, second(i, j))),
            pl.BlockSpec((1, d), lambda i, j: (0, 0)),
        ],
        out_specs=pl.BlockSpec((tm, d), lambda i, j: (i, 0)),
        scratch_shapes=[pltpu.VMEM((tm, d), BF16), pltpu.VMEM((tm, 1), F32),
                        pltpu.VMEM((tm, 1), F32)],
        compiler_params=_cparams(2),
        name="out_block",
    )(x2, y_a, y_c, w_cat_bf, ple_g.reshape(1, d), b_gate.reshape(1, d), p2, w_ple_bf,
      final_g.reshape(1, d))


def kernel(x, p, norm_in_g, w_in, rel_table, w_dw, b_dw, conv_ln_g, conv_ln_b, w_pw, b_pw,
           attn_out_g, conv_out_g, w_out, ple_norm_g, w_ple_gate, b_ple_gate, w_ple, final_g):
    batch, seq, d = x.shape
    depth = w_in.shape[0]
    conv_width = w_pw.shape[1]
    att_width = d - conv_width
    m = batch * seq
    assert depth >= 1
    assert seq % Q_TILE == 0 and Q_TILE % CHUNK == 0
    assert (KV_BLOCKS - 1) * Q_TILE == LEFT_CHUNKS * CHUNK
    assert att_width == conv_width
    head_dim = att_width // N_HEADS
    col_scale = jnp.where(jnp.arange(w_in.shape[2]) < att_width,
                          head_dim ** -0.5 * LOG2_E, 1.0).astype(F32)

    tm = min(512, m)
    h = x.reshape(m, d)
    for i in range(depth):
        last = i == depth - 1
        proj = _in_projection(h, norm_in_g[i], (w_in[i] * col_scale).astype(BF16),
                              tm=tm, tn=min(1024, d))
        y_a = _attention(proj, _bias_rows(rel_table[i]), attn_out_g[i],
                         batch=batch, seq=seq, width=att_width, n_heads=N_HEADS)
        y_c = _conv_module(proj, w_dw[i], b_dw[i], conv_ln_g[i], conv_ln_b[i],
                           w_pw[i].astype(BF16), b_pw[i], conv_out_g[i],
                           batch=batch, seq=seq, width=conv_width, col0=4, tt=min(256, seq))
        w_cat = jnp.concatenate([w_out[i], w_ple_gate[i]], axis=1).astype(BF16)
        h = _out_block(h, y_a, y_c, w_cat, ple_norm_g[i], b_ple_gate[i], p[i].reshape(m, -1),
                       w_ple[i].astype(BF16), final_g, tm=tm, tn=min(1024, d), apply_final=last)
    return h.reshape(batch, seq, d)
```

```python
import functools

import numpy as np
import jax
import jax.numpy as jnp
from jax import lax
from jax.experimental import pallas as pl
from jax.experimental.pallas import tpu as pltpu

CHUNK = 64
LEFT_CHUNKS = 8
N_HEADS = 16
MAX_REL = 256
CONV_K = 31
EPS = 1e-6
NEG_INF = -1e30
LOG2_E = 1.4426950408889634

LANES = 128
Q_TILE = 256
KV_BLOCKS = 3
BAND_KEYS = KV_BLOCKS * Q_TILE
BIAS_ROW = 1024
HALO = 32
ROW_CHUNK = 32
VMEM_LIMIT = 60 * 1024 * 1024

F32 = jnp.float32
BF16 = jnp.bfloat16


def _cparams(n_axes, flags=None):
    return pltpu.CompilerParams(
        dimension_semantics=("arbitrary",) * n_axes, vmem_limit_bytes=VMEM_LIMIT, flags=flags)


def _row_loop(n_rows, body, unroll=1):
    def step(r, carry):
        body(pl.ds(pl.multiple_of(r * ROW_CHUNK, ROW_CHUNK), ROW_CHUNK))
        return carry
    lax.fori_loop(0, n_rows // ROW_CHUNK, step, 0, unroll=unroll)


def _silu(z):
    return z * jax.nn.sigmoid(z)


def _proj_kernel(x_ref, g_ref, w_ref, wdw_ref, bdw_ref, lng_ref, lnb_ref, o_ref, yb_ref,
                 xn_ref, ag_ref, u2_ref, halo_ref, c_ref, *, n_glu, n_conv, blocks_per_seq):
    i = pl.program_id(0)
    j = pl.program_id(1)
    tm = x_ref.shape[0]
    tn = w_ref.shape[1]
    cw = c_ref.shape[1]
    half = tm // 2

    @pl.when(j == 0)
    def _():
        def norm_rows(rows):
            x = x_ref[rows, :]
            ms = jnp.mean(x * x, axis=-1, keepdims=True)
            xn_ref[rows, :] = (x * lax.rsqrt(ms + EPS) * g_ref[...]).astype(BF16)
        _row_loop(tm, norm_rows)

    @pl.when((i == 0) & (j == 0))
    def _():
        u2_ref[...] = jnp.zeros_like(u2_ref)
        halo_ref[...] = jnp.zeros_like(halo_ref)

    def project():
        return jnp.dot(xn_ref[...], w_ref[...], preferred_element_type=F32).astype(BF16)

    def project_quarter(q):
        cols = slice(q * (tn // 4), (q + 1) * (tn // 4))
        o_ref[:, cols] = jnp.dot(xn_ref[...], w_ref[:, cols],
                                 preferred_element_type=F32).astype(BF16)

    @pl.when(j < n_glu)
    def _():
        ag_ref[:, pl.ds(pl.multiple_of(j * tn, tn), tn)] = project()

    @pl.when((j >= n_glu) & (j < n_glu + n_conv))
    def _():
        seq_start = (i % blocks_per_seq) == 0
        for slot in range(2):
            c = 2 * (j - n_glu) + slot
            va = pl.ds(pl.multiple_of(c * LANES, LANES), LANES)
            ga = pl.ds(pl.multiple_of(cw + c * LANES, LANES), LANES)
            u2_ref[slot, 0:2 * HALO, :] = jnp.where(seq_start, 0.0, halo_ref[i % 2, c])
            u = ag_ref[:, va].astype(F32) * jax.nn.sigmoid(ag_ref[:, ga].astype(F32))
            u2_ref[slot, pl.ds(2 * HALO, tm, stride=2), :] = u
            halo_ref[(i + 1) % 2, c] = u2_ref[slot, 2 * tm:2 * (tm + HALO), :]
            for s, r0 in enumerate(range(0, tm, LANES)):
                acc = jnp.zeros((LANES, LANES), F32)
                for k in range(CONV_K):
                    first = r0 + HALO - (CONV_K - 1) + k
                    acc += u2_ref[slot, pl.ds(2 * first, LANES, stride=2), :] * wdw_ref[k:k + 1, va]
                c_ref[r0:r0 + LANES, va] = acc + bdw_ref[:, va]
                if s % 2 == 1:
                    project_quarter(2 * slot + s // 2)

    @pl.when(j >= n_glu + n_conv)
    def _():
        base = (j - (n_glu + n_conv)) * half
        n_chunks = half // ROW_CHUNK
        for r in range(n_chunks):
            rows = pl.ds(pl.multiple_of(base + r * ROW_CHUNK, ROW_CHUNK), ROW_CHUNK)
            c = c_ref[rows, :]
            mu = jnp.mean(c, axis=-1, keepdims=True)
            cc = c - mu
            var = jnp.mean(cc * cc, axis=-1, keepdims=True)
            y = cc * lax.rsqrt(var + EPS) * lng_ref[...] + lnb_ref[...]
            yb_ref[rows, :] = _silu(y).astype(BF16)
            if r % (n_chunks // 4) == n_chunks // 4 - 1:
                project_quarter(r // (n_chunks // 4))


def _in_projection(x2, g, w_bf, w_dw, b_dw, ln_g, ln_b, *, tm, tn, conv_width, glu_block0, seq):
    m, d = x2.shape
    n = w_bf.shape[1]
    nj = n // tn
    n_glu = 2 * conv_width // tn
    n_conv = conv_width // (2 * LANES)
    assert nj == n_glu + n_conv + 2 and tm % (2 * ROW_CHUNK) == 0 and seq % tm == 0
    w_dw_p = jnp.zeros((HALO, conv_width), F32).at[:CONV_K].set(w_dw)

    def w_block(i, j):
        rest = j - n_glu
        return (0, jnp.where(j < n_glu, glu_block0 + j,
                             jnp.where(rest < glu_block0, rest, rest + n_glu)))

    def vec():
        return pl.BlockSpec((1, conv_width), lambda i, j: (0, 0))

    return pl.pallas_call(
        functools.partial(_proj_kernel, n_glu=n_glu, n_conv=n_conv, blocks_per_seq=seq // tm),
        out_shape=(jax.ShapeDtypeStruct((m, n - n_glu * tn), BF16),
                   jax.ShapeDtypeStruct((m, conv_width), BF16)),
        grid=(m // tm, nj),
        in_specs=[
            pl.BlockSpec((tm, d), lambda i, j: (i, 0)),
            pl.BlockSpec((1, d), lambda i, j: (0, 0)),
            pl.BlockSpec((d, tn), w_block),
            pl.BlockSpec((HALO, conv_width), lambda i, j: (0, 0)),
            vec(), vec(), vec(),
        ],
        out_specs=(pl.BlockSpec((tm, tn), lambda i, j: (i, jnp.maximum(j - n_glu, 0))),
                   pl.BlockSpec((tm, conv_width), lambda i, j: (i, 0))),
        scratch_shapes=[pltpu.VMEM((tm, d), BF16),
                        pltpu.VMEM((tm, 2 * conv_width), BF16),
                        pltpu.VMEM((2, 2 * (tm + HALO), LANES), F32),
                        pltpu.VMEM((2, conv_width // LANES, 2 * HALO, LANES), F32),
                        pltpu.VMEM((tm, conv_width), F32)],
        compiler_params=_cparams(2),
        name="in_projection",
    )(x2, g.reshape(1, d), w_bf, w_dw_p, b_dw.reshape(1, conv_width),
      ln_g.reshape(1, conv_width), ln_b.reshape(1, conv_width))


def _attn_kernel(q_ref, k0_ref, k1_ref, k2_ref, v0_ref, v1_ref, v2_ref, z_ref, brow_ref, g_ref,
                 o_ref, y_ref, bias_ref, vext_ref, *, n_heads, head_dim):
    t = pl.program_id(1)
    k_refs = (k0_ref, k1_ref, k2_ref)
    v_refs = (v0_ref, v1_ref, v2_ref)

    @pl.when((pl.program_id(0) == 0) & (t == 0))
    def _():
        qi = lax.broadcasted_iota(jnp.int32, (Q_TILE, BAND_KEYS), 0) // CHUNK
        kj = lax.broadcasted_iota(jnp.int32, (Q_TILE, BAND_KEYS), 1) // CHUNK
        in_band = (kj >= qi) & (kj <= qi + LEFT_CHUNKS)
        for h in range(n_heads):
            row = jnp.broadcast_to(brow_ref[h:h + 1, :], (Q_TILE, BIAS_ROW))
            skew = pltpu.roll(row, 0, 1, stride=1, stride_axis=0)
            bias_ref[h, :, 0:BAND_KEYS] = jnp.where(in_band, skew[:, :BAND_KEYS], NEG_INF)
            bias_ref[h, :, BAND_KEYS:] = jnp.full((Q_TILE, Q_TILE), NEG_INF, F32)
        vext_ref[:, :, head_dim:] = jnp.ones((n_heads, BAND_KEYS, head_dim), BF16)

    for h in range(n_heads):
        cols = slice(h * head_dim, (h + 1) * head_dim)
        q = q_ref[:, cols]
        s = []
        for j in range(KV_BLOCKS):
            vext_ref[h, j * Q_TILE:(j + 1) * Q_TILE, 0:head_dim] = v_refs[j][:, cols]
            sj = lax.dot_general(q, k_refs[j][:, cols], (((1,), (1,)), ((), ())),
                                 preferred_element_type=F32)
            if j < KV_BLOCKS - 1:
                valid = t + (j - (KV_BLOCKS - 1)) >= 0
                off = pl.multiple_of(jnp.where(valid, j * Q_TILE, BAND_KEYS), Q_TILE)
            else:
                off = j * Q_TILE
            s.append(sj + bias_ref[h, :, pl.ds(off, Q_TILE)])
        m = jnp.max(jnp.maximum(jnp.maximum(s[0], s[1]), s[2]), axis=-1, keepdims=True)
        oe = None
        for j in range(KV_BLOCKS):
            pj = jnp.exp2(s[j] - m).astype(BF16)
            part = jnp.dot(pj, vext_ref[h, j * Q_TILE:(j + 1) * Q_TILE, :],
                           preferred_element_type=F32)
            oe = part if oe is None else oe + part
        y_ref[:, cols] = oe[:, 0:head_dim] / oe[:, head_dim:]

    def finish_rows(rows):
        y = y_ref[rows, :]
        ms = jnp.mean(y * y, axis=-1, keepdims=True)
        z = z_ref[rows, :].astype(F32)
        o_ref[rows, :] = (y * lax.rsqrt(ms + EPS) * g_ref[...] * _silu(z)).astype(o_ref.dtype)
    _row_loop(Q_TILE, finish_rows)


def _bias_rows(table):
    e = np.arange(BIAS_ROW)
    e = np.where(e >= BAND_KEYS, e - BIAS_ROW, e)
    idx = np.clip((KV_BLOCKS - 1) * Q_TILE - e, -MAX_REL, MAX_REL) + MAX_REL
    return table[:, idx].astype(F32) * LOG2_E


def _attention(proj, brow, g, *, batch, seq, width, n_heads):
    m = proj.shape[0]
    tiles = seq // Q_TILE
    head_dim = width // n_heads

    def spec(off, col):
        return pl.BlockSpec((Q_TILE, width),
                            lambda b, t: (b * tiles + jnp.maximum(t + off, 0), col))

    in_specs = [spec(0, 0),
                spec(-2, 1), spec(-1, 1), spec(0, 1),
                spec(-2, 2), spec(-1, 2), spec(0, 2),
                spec(0, 3),
                pl.BlockSpec(brow.shape, lambda b, t: (0, 0)),
                pl.BlockSpec((1, width), lambda b, t: (0, 0))]
    return pl.pallas_call(
        functools.partial(_attn_kernel, n_heads=n_heads, head_dim=head_dim),
        out_shape=jax.ShapeDtypeStruct((m, width), BF16),
        grid=(batch, tiles),
        in_specs=in_specs,
        out_specs=pl.BlockSpec((Q_TILE, width), lambda b, t: (b * tiles + t, 0)),
        scratch_shapes=[pltpu.VMEM((Q_TILE, width), F32),
                        pltpu.VMEM((n_heads, Q_TILE, BAND_KEYS + Q_TILE), F32),
                        pltpu.VMEM((n_heads, BAND_KEYS, 2 * head_dim), BF16)],
        compiler_params=_cparams(2),
        name="band_attention",
    )(*([proj] * 8), brow, g.reshape(1, width))


def _conv_out_kernel(act_ref, z_ref, wpw_ref, bpw_ref, g_ref, o_ref, pw_ref):
    pw_ref[...] = jnp.dot(act_ref[...], wpw_ref[...], preferred_element_type=F32)

    def finish_rows(rows):
        pw = pw_ref[rows, :] + bpw_ref[...]
        ms = jnp.mean(pw * pw, axis=-1, keepdims=True)
        z = z_ref[rows, :].astype(F32)
        o_ref[rows, :] = (pw * lax.rsqrt(ms + EPS) * g_ref[...] * _silu(z)).astype(o_ref.dtype)
    _row_loop(act_ref.shape[0], finish_rows)


def _conv_module(act, proj, w_pw_bf, b_pw, g, *, z_col, tt):
    m, width = act.shape

    def vec():
        return pl.BlockSpec((1, width), lambda t: (0, 0))

    return pl.pallas_call(
        _conv_out_kernel,
        out_shape=jax.ShapeDtypeStruct((m, width), BF16),
        grid=(m // tt,),
        in_specs=[pl.BlockSpec((tt, width), lambda t: (t, 0)),
                  pl.BlockSpec((tt, width), lambda t: (t, z_col)),
                  pl.BlockSpec((width, width), lambda t: (0, 0)),
                  vec(), vec()],
        out_specs=pl.BlockSpec((tt, width), lambda t: (t, 0)),
        scratch_shapes=[pltpu.VMEM((tt, width), F32)],
        compiler_params=_cparams(1),
        name="conv_module",
    )(act, proj, w_pw_bf, b_pw.reshape(1, width), g.reshape(1, width))


def _out_kernel(x_ref, ya_ref, yc_ref, w_ref, pg_ref, bg_ref, p_ref, wp_ref, fg_ref,
                o_ref, hn_ref, ssh_ref, sso_ref, *, nj, tn, apply_final):
    j = pl.program_id(1)
    wa = ya_ref.shape[1]
    d = o_ref.shape[1]

    @pl.when(j == 0)
    def _():
        ssh_ref[...] = jnp.zeros_like(ssh_ref)
        sso_ref[...] = jnp.zeros_like(sso_ref)

    @pl.when(j < nj)
    def _():
        cols = pl.ds(pl.multiple_of(j * tn, tn), tn)
        acc = jnp.dot(ya_ref[...], w_ref[0:wa, :], preferred_element_type=F32)
        acc += jnp.dot(yc_ref[...], w_ref[wa:, :], preferred_element_type=F32)
        h = x_ref[...] + acc
        o_ref[:, cols] = h
        hn_ref[:, cols] = (h * pg_ref[:, cols]).astype(BF16)
        ssh_ref[...] += jnp.sum(h * h, axis=-1, keepdims=True)

    @pl.when(j >= nj)
    def _():
        cols = pl.ds(pl.multiple_of((j - nj) * tn, tn), tn)
        rinv = lax.rsqrt(ssh_ref[...] * (1.0 / d) + EPS)
        gate = jax.nn.sigmoid(
            jnp.dot(hn_ref[...], w_ref[...], preferred_element_type=F32) * rinv + bg_ref[...])
        pe = jnp.dot(p_ref[...].astype(BF16), wp_ref[...], preferred_element_type=F32)
        h2 = o_ref[:, cols] + gate * pe
        o_ref[:, cols] = h2
        sso_ref[...] += jnp.sum(h2 * h2, axis=-1, keepdims=True)

    if apply_final:
        @pl.when(j == 2 * nj - 1)
        def _():
            def final_rows(rows):
                rinv = lax.rsqrt(sso_ref[rows, :] * (1.0 / d) + EPS)
                o_ref[rows, :] = o_ref[rows, :] * rinv * fg_ref[...]
            _row_loop(o_ref.shape[0], final_rows)


def _out_block(x2, y_a, y_c, w_cat_bf, ple_g, b_gate, p2, w_ple_bf, final_g, *, tm, tn, apply_final):
    m, d = x2.shape
    wa = y_a.shape[1]
    wc = y_c.shape[1]
    pd = p2.shape[1]
    nj = d // tn

    def first(i, j):
        return jnp.minimum(j, nj - 1)

    def second(i, j):
        return jnp.maximum(j - nj, 0)

    return pl.pallas_call(
        functools.partial(_out_kernel, nj=nj, tn=tn, apply_final=apply_final),
        out_shape=jax.ShapeDtypeStruct((m, d), F32),
        grid=(m // tm, 2 * nj),
        in_specs=[
            pl.BlockSpec((tm, tn), lambda i, j: (i, first(i, j))),
            pl.BlockSpec((tm, wa), lambda i, j: (i, 0)),
            pl.BlockSpec((tm, wc), lambda i, j: (i, 0)),
            pl.BlockSpec((d, tn), lambda i, j: (0, j)),
            pl.BlockSpec((1, d), lambda i, j: (0, 0)),
            pl.BlockSpec((1, tn), lambda i, j: (0, second(i, j))),
            pl.BlockSpec((tm, pd), lambda i, j: (i, 0)),
            pl.BlockSpec((pd, tn), lambda i, j: (0, second(i, j))),
            pl.BlockSpec((1, d), lambda i, j: (0, 0)),
        ],
        out_specs=pl.BlockSpec((tm, d), lambda i, j: (i, 0)),
        scratch_shapes=[pltpu.VMEM((tm, d), BF16), pltpu.VMEM((tm, 1), F32),
                        pltpu.VMEM((tm, 1), F32)],
        compiler_params=_cparams(2),
        name="out_block",
    )(x2, y_a, y_c, w_cat_bf, ple_g.reshape(1, d), b_gate.reshape(1, d), p2, w_ple_bf,
      final_g.reshape(1, d))


def kernel(x, p, norm_in_g, w_in, rel_table, w_dw, b_dw, conv_ln_g, conv_ln_b, w_pw, b_pw,
           attn_out_g, conv_out_g, w_out, ple_norm_g, w_ple_gate, b_ple_gate, w_ple, final_g):
    batch, seq, d = x.shape
    depth = w_in.shape[0]
    conv_width = w_pw.shape[1]
    att_width = d - conv_width
    m = batch * seq
    assert depth >= 1
    assert seq % Q_TILE == 0 and Q_TILE % CHUNK == 0
    assert (KV_BLOCKS - 1) * Q_TILE == LEFT_CHUNKS * CHUNK
    assert att_width == conv_width
    head_dim = att_width // N_HEADS
    col_scale = jnp.where(jnp.arange(w_in.shape[2]) < att_width,
                          head_dim ** -0.5 * LOG2_E, 1.0).astype(F32)

    tm = min(512, m)
    tn = 1024
    glu_block0 = 4 * att_width // tn
    h = x.reshape(m, d)
    for i in range(depth):
        last = i == depth - 1
        proj, act = _in_projection(h, norm_in_g[i], (w_in[i] * col_scale).astype(BF16),
                                   w_dw[i], b_dw[i], conv_ln_g[i], conv_ln_b[i],
                                   tm=tm, tn=tn, conv_width=conv_width, glu_block0=glu_block0, seq=seq)
        y_a = _attention(proj, _bias_rows(rel_table[i]), attn_out_g[i],
                         batch=batch, seq=seq, width=att_width, n_heads=N_HEADS)
        y_c = _conv_module(act, proj, w_pw[i].astype(BF16), b_pw[i], conv_out_g[i], z_col=4, tt=tm)
        w_cat = jnp.concatenate([w_out[i], w_ple_gate[i]], axis=1).astype(BF16)
        h = _out_block(h, y_a, y_c, w_cat, ple_norm_g[i], b_ple_gate[i], p[i].reshape(m, -1),
                       w_ple[i].astype(BF16), final_g, tm=tm, tn=min(1024, d), apply_final=last)
    return h.reshape(batch, seq, d)
```

```python
import functools

import numpy as np
import jax
import jax.numpy as jnp
from jax import lax
from jax.experimental import pallas as pl
from jax.experimental.pallas import tpu as pltpu

CHUNK = 64
LEFT_CHUNKS = 8
N_HEADS = 16
MAX_REL = 256
CONV_K = 31
EPS = 1e-6
NEG_INF = -1e30
LOG2_E = 1.4426950408889634

LANES = 128
Q_TILE = 256
KV_BLOCKS = 3
BAND_KEYS = KV_BLOCKS * Q_TILE
BIAS_ROW = 1024
HALO = 32
ROW_CHUNK = 32
VMEM_LIMIT = 56 * 1024 * 1024

F32 = jnp.float32
BF16 = jnp.bfloat16


def _cparams(n_axes):
    return pltpu.CompilerParams(
        dimension_semantics=("arbitrary",) * n_axes, vmem_limit_bytes=VMEM_LIMIT)


def _row_loop(n_rows, body, unroll=1):
    def step(r, carry):
        body(pl.ds(pl.multiple_of(r * ROW_CHUNK, ROW_CHUNK), ROW_CHUNK))
        return carry
    lax.fori_loop(0, n_rows // ROW_CHUNK, step, 0, unroll=unroll)


def _silu(z):
    return z * jax.nn.sigmoid(z)


def _proj_kernel(x_hbm, g_ref, w_ref, o_ref, x_buf, xn_ref, sem):
    i = pl.program_id(0)
    j = pl.program_id(1)
    tm = x_buf.shape[0]

    def x_copy(block):
        rows = pl.ds(pl.multiple_of(block * tm, tm), tm)
        return pltpu.make_async_copy(x_hbm.at[rows, :], x_buf, sem)

    @pl.when((i == 0) & (j == 0))
    def _():
        x_copy(0).start()

    @pl.when(j == 0)
    def _():
        x_copy(i).wait()

        def norm_rows(rows):
            x = x_buf[rows, :]
            ms = jnp.mean(x * x, axis=-1, keepdims=True)
            xn_ref[rows, :] = (x * lax.rsqrt(ms + EPS) * g_ref[...]).astype(BF16)
        _row_loop(tm, norm_rows, unroll=2)

    @pl.when((j == 1) & (i + 1 < pl.num_programs(0)))
    def _():
        x_copy(i + 1).start()

    o_ref[...] = jnp.dot(xn_ref[...], w_ref[...], preferred_element_type=F32).astype(o_ref.dtype)


def _in_projection(x2, g, w_bf, *, tm, tn):
    m, d = x2.shape
    n = w_bf.shape[1]
    assert n // tn >= 2
    return pl.pallas_call(
        _proj_kernel,
        out_shape=jax.ShapeDtypeStruct((m, n), BF16),
        grid=(m // tm, n // tn),
        in_specs=[
            pl.BlockSpec(memory_space=pl.ANY),
            pl.BlockSpec((1, d), lambda i, j: (0, 0)),
            pl.BlockSpec((d, tn), lambda i, j: (0, j)),
        ],
        out_specs=pl.BlockSpec((tm, tn), lambda i, j: (i, j)),
        scratch_shapes=[pltpu.VMEM((tm, d), F32), pltpu.VMEM((tm, d), BF16),
                        pltpu.SemaphoreType.DMA(())],
        compiler_params=_cparams(2),
        name="in_projection",
    )(x2, g.reshape(1, d), w_bf)


def _attn_kernel(q_ref, k0_ref, k1_ref, k2_ref, v0_ref, v1_ref, v2_ref, z_ref, brow_ref, g_ref,
                 o_ref, y_ref, bias_ref, vext_ref, *, n_heads, head_dim):
    t = pl.program_id(1)
    k_refs = (k0_ref, k1_ref, k2_ref)
    v_refs = (v0_ref, v1_ref, v2_ref)

    @pl.when((pl.program_id(0) == 0) & (t == 0))
    def _():
        qi = lax.broadcasted_iota(jnp.int32, (Q_TILE, BAND_KEYS), 0) // CHUNK
        kj = lax.broadcasted_iota(jnp.int32, (Q_TILE, BAND_KEYS), 1) // CHUNK
        in_band = (kj >= qi) & (kj <= qi + LEFT_CHUNKS)
        for h in range(n_heads):
            row = jnp.broadcast_to(brow_ref[h:h + 1, :], (Q_TILE, BIAS_ROW))
            skew = pltpu.roll(row, 0, 1, stride=1, stride_axis=0)
            bias_ref[h, :, 0:BAND_KEYS] = jnp.where(in_band, skew[:, :BAND_KEYS], NEG_INF)
            bias_ref[h, :, BAND_KEYS:] = jnp.full((Q_TILE, Q_TILE), NEG_INF, F32)
        vext_ref[:, :, head_dim:] = jnp.ones((n_heads, BAND_KEYS, head_dim), BF16)

    ss = jnp.zeros((Q_TILE, 1), F32)
    for h in range(n_heads):
        cols = slice(h * head_dim, (h + 1) * head_dim)
        q = q_ref[:, cols]
        s = []
        for j in range(KV_BLOCKS):
            vext_ref[h, j * Q_TILE:(j + 1) * Q_TILE, 0:head_dim] = v_refs[j][:, cols]
            sj = lax.dot_general(q, k_refs[j][:, cols], (((1,), (1,)), ((), ())),
                                 preferred_element_type=F32)
            if j < KV_BLOCKS - 1:
                valid = t + (j - (KV_BLOCKS - 1)) >= 0
                off = pl.multiple_of(jnp.where(valid, j * Q_TILE, BAND_KEYS), Q_TILE)
            else:
                off = j * Q_TILE
            s.append(sj + bias_ref[h, :, pl.ds(off, Q_TILE)])
        m = jnp.max(jnp.maximum(jnp.maximum(s[0], s[1]), s[2]), axis=-1, keepdims=True)
        oe = None
        for j in range(KV_BLOCKS):
            pj = jnp.exp2(s[j] - m).astype(BF16)
            part = jnp.dot(pj, vext_ref[h, j * Q_TILE:(j + 1) * Q_TILE, :],
                           preferred_element_type=F32)
            oe = part if oe is None else oe + part
        y = oe[:, 0:head_dim] / oe[:, head_dim:]
        ss = ss + jnp.sum(y * y, axis=-1, keepdims=True)
        y_ref[:, cols] = y * (g_ref[:, cols] * _silu(z_ref[:, cols].astype(F32)))

    rinv = lax.rsqrt(ss * (1.0 / (n_heads * head_dim)) + EPS)
    for r0 in range(0, Q_TILE, ROW_CHUNK):
        rows = slice(r0, r0 + ROW_CHUNK)
        o_ref[rows, :] = (y_ref[rows, :] * rinv[rows, :]).astype(o_ref.dtype)


def _bias_rows(table):
    e = np.arange(BIAS_ROW)
    e = np.where(e >= BAND_KEYS, e - BIAS_ROW, e)
    idx = np.clip((KV_BLOCKS - 1) * Q_TILE - e, -MAX_REL, MAX_REL) + MAX_REL
    return table[:, idx].astype(F32) * LOG2_E


def _attention(proj, brow, g, *, batch, seq, width, n_heads):
    m = proj.shape[0]
    tiles = seq // Q_TILE
    head_dim = width // n_heads

    def spec(off, col):
        return pl.BlockSpec((Q_TILE, width),
                            lambda b, t: (b * tiles + jnp.maximum(t + off, 0), col))

    in_specs = [spec(0, 0),
                spec(-2, 1), spec(-1, 1), spec(0, 1),
                spec(-2, 2), spec(-1, 2), spec(0, 2),
                spec(0, 3),
                pl.BlockSpec(brow.shape, lambda b, t: (0, 0)),
                pl.BlockSpec((1, width), lambda b, t: (0, 0))]
    return pl.pallas_call(
        functools.partial(_attn_kernel, n_heads=n_heads, head_dim=head_dim),
        out_shape=jax.ShapeDtypeStruct((m, width), BF16),
        grid=(batch, tiles),
        in_specs=in_specs,
        out_specs=pl.BlockSpec((Q_TILE, width), lambda b, t: (b * tiles + t, 0)),
        scratch_shapes=[pltpu.VMEM((Q_TILE, width), F32),
                        pltpu.VMEM((n_heads, Q_TILE, BAND_KEYS + Q_TILE), F32),
                        pltpu.VMEM((n_heads, BAND_KEYS, 2 * head_dim), BF16)],
        compiler_params=_cparams(2),
        name="band_attention",
    )(*([proj] * 8), brow, g.reshape(1, width))


def _conv_kernel(a_ref, gl_ref, z_ref, wdw_ref, bdw_ref, lng_ref, lnb_ref, wpw_ref, bpw_ref, g_ref,
                 o_ref, u2_ref, c_ref, yb_ref, pw_ref, *, strip_rows):
    tt, width = a_ref.shape
    n_slabs = width // LANES
    t = pl.program_id(1)

    @pl.when(t == 0)
    def _():
        u2_ref[...] = jnp.zeros_like(u2_ref)

    @pl.when(t > 0)
    def _():
        u2_ref[:, 0:2 * HALO, :] = u2_ref[:, 2 * tt:2 * (tt + HALO), :]

    def glu_rows(rows):
        u = a_ref[rows, :].astype(F32) * jax.nn.sigmoid(gl_ref[rows, :].astype(F32))
        for c in range(n_slabs):
            u2_ref[c, pl.ds(2 * (HALO + rows.start), ROW_CHUNK, stride=2), :] = (
                u[:, c * LANES:(c + 1) * LANES])
    _row_loop(tt, glu_rows)

    def conv_slab(c, carry):
        cols = pl.ds(pl.multiple_of(c * LANES, LANES), LANES)
        for r0 in range(0, tt, strip_rows):
            acc = jnp.zeros((strip_rows, LANES), F32)
            for k in range(CONV_K):
                first = r0 + HALO - (CONV_K - 1) + k
                acc += u2_ref[c, pl.ds(2 * first, strip_rows, stride=2), :] * wdw_ref[k:k + 1, cols]
            c_ref[r0:r0 + strip_rows, cols] = acc + bdw_ref[:, cols]
        return carry
    lax.fori_loop(0, n_slabs, conv_slab, 0)

    def norm_rows(rows):
        c = c_ref[rows, :]
        mu = jnp.mean(c, axis=-1, keepdims=True)
        cc = c - mu
        var = jnp.mean(cc * cc, axis=-1, keepdims=True)
        y = cc * lax.rsqrt(var + EPS) * lng_ref[...] + lnb_ref[...]
        yb_ref[rows, :] = _silu(y).astype(BF16)
    _row_loop(tt, norm_rows, unroll=2)

    pw_ref[...] = jnp.dot(yb_ref[...], wpw_ref[...], preferred_element_type=F32)

    def finish_rows(rows):
        pw = pw_ref[rows, :] + bpw_ref[...]
        ms = jnp.mean(pw * pw, axis=-1, keepdims=True)
        z = z_ref[rows, :].astype(F32)
        o_ref[rows, :] = (pw * lax.rsqrt(ms + EPS) * g_ref[...] * _silu(z)).astype(o_ref.dtype)
    _row_loop(tt, finish_rows, unroll=2)


def _conv_module(proj, w_dw, b_dw, ln_g, ln_b, w_pw_bf, b_pw, g, *, batch, seq, width, col0, tt):
    m = proj.shape[0]
    tiles = seq // tt
    w_dw_p = jnp.zeros((HALO, width), F32).at[:CONV_K].set(w_dw)

    def spec(col):
        return pl.BlockSpec((tt, width), lambda b, t: (b * tiles + t, col))

    def vec():
        return pl.BlockSpec((1, width), lambda b, t: (0, 0))

    return pl.pallas_call(
        functools.partial(_conv_kernel, strip_rows=128),
        out_shape=jax.ShapeDtypeStruct((m, width), BF16),
        grid=(batch, tiles),
        in_specs=[spec(col0), spec(col0 + 1), spec(col0 + 2),
                  pl.BlockSpec((HALO, width), lambda b, t: (0, 0)),
                  vec(), vec(), vec(),
                  pl.BlockSpec((width, width), lambda b, t: (0, 0)),
                  vec(), vec()],
        out_specs=pl.BlockSpec((tt, width), lambda b, t: (b * tiles + t, 0)),
        scratch_shapes=[pltpu.VMEM((width // LANES, 2 * (tt + HALO), LANES), F32),
                        pltpu.VMEM((tt, width), F32),
                        pltpu.VMEM((tt, width), BF16),
                        pltpu.VMEM((tt, width), F32)],
        compiler_params=_cparams(2),
        name="conv_module",
    )(proj, proj, proj, w_dw_p, b_dw.reshape(1, width), ln_g.reshape(1, width),
      ln_b.reshape(1, width), w_pw_bf, b_pw.reshape(1, width), g.reshape(1, width))


def _out_kernel(x_ref, ya_ref, yc_ref, w_ref, pg_ref, bg_ref, p_ref, wp_ref, fg_ref,
                o_ref, hn_ref, ssh_ref, sso_ref, *, nj, tn, apply_final):
    j = pl.program_id(1)
    wa = ya_ref.shape[1]
    d = o_ref.shape[1]

    @pl.when(j == 0)
    def _():
        ssh_ref[...] = jnp.zeros_like(ssh_ref)
        sso_ref[...] = jnp.zeros_like(sso_ref)

    @pl.when(j < nj)
    def _():
        cols = pl.ds(pl.multiple_of(j * tn, tn), tn)
        acc = jnp.dot(ya_ref[...], w_ref[0:wa, :], preferred_element_type=F32)
        acc += jnp.dot(yc_ref[...], w_ref[wa:, :], preferred_element_type=F32)
        h = x_ref[...] + acc
        o_ref[:, cols] = h
        hn_ref[:, cols] = (h * pg_ref[:, cols]).astype(BF16)
        ssh_ref[...] += jnp.sum(h * h, axis=-1, keepdims=True)

    @pl.when(j >= nj)
    def _():
        cols = pl.ds(pl.multiple_of((j - nj) * tn, tn), tn)
        rinv = lax.rsqrt(ssh_ref[...] * (1.0 / d) + EPS)
        gate = jax.nn.sigmoid(
            jnp.dot(hn_ref[...], w_ref[...], preferred_element_type=F32) * rinv + bg_ref[...])
        pe = jnp.dot(p_ref[...].astype(BF16), wp_ref[...], preferred_element_type=F32)
        h2 = o_ref[:, cols] + gate * pe
        o_ref[:, cols] = h2
        sso_ref[...] += jnp.sum(h2 * h2, axis=-1, keepdims=True)

    if apply_final:
        @pl.when(j == 2 * nj - 1)
        def _():
            def final_rows(rows):
                rinv = lax.rsqrt(sso_ref[rows, :] * (1.0 / d) + EPS)
                o_ref[rows, :] = o_ref[rows, :] * rinv * fg_ref[...]
            _row_loop(o_ref.shape[0], final_rows, unroll=2)


def _out_block(x2, y_a, y_c, w_pair_bf, ple_g, b_gate, p2, w_ple_bf, final_g, *, tm, tn, apply_final):
    m, d = x2.shape
    wa = y_a.shape[1]
    wc = y_c.shape[1]
    pd = p2.shape[1]
    nj = d // tn

    def first(i, j):
        return jnp.minimum(j, nj - 1)

    def second(i, j):
        return jnp.maximum(j - nj, 0)

    return pl.pallas_call(
        functools.partial(_out_kernel, nj=nj, tn=tn, apply_final=apply_final),
        out_shape=jax.ShapeDtypeStruct((m, d), F32),
        grid=(m // tm, 2 * nj),
        in_specs=[
            pl.BlockSpec((tm, tn), lambda i, j: (i, first(i, j))),
            pl.BlockSpec((tm, wa), lambda i, j: (i, 0)),
            pl.BlockSpec((tm, wc), lambda i, j: (i, 0)),
            pl.BlockSpec((None, d, tn), lambda i, j: (j // nj, 0, j % nj)),
            pl.BlockSpec((1, d), lambda i, j: (0, 0)),
            pl.BlockSpec((1, tn), lambda i, j: (0, second(i, j))),
            pl.BlockSpec((tm, pd), lambda i, j: (i, 0)),
            pl.BlockSpec((pd, tn), lambda i, j: (0, second(i, j))),
            pl.BlockSpec((1, d), lambda i, j: (0, 0)),
        ],
        out_specs=pl.BlockSpec((tm, d), lambda i, j: (i, 0)),
        scratch_shapes=[pltpu.VMEM((tm, d), BF16), pltpu.VMEM((tm, 1), F32),
                        pltpu.VMEM((tm, 1), F32)],
        compiler_params=_cparams(2),
        name="out_block",
    )(x2, y_a, y_c, w_pair_bf, ple_g.reshape(1, d), b_gate.reshape(1, d), p2, w_ple_bf,
      final_g.reshape(1, d))


def kernel(x, p, norm_in_g, w_in, rel_table, w_dw, b_dw, conv_ln_g, conv_ln_b, w_pw, b_pw,
           attn_out_g, conv_out_g, w_out, ple_norm_g, w_ple_gate, b_ple_gate, w_ple, final_g):
    batch, seq, d = x.shape
    depth = w_in.shape[0]
    conv_width = w_pw.shape[1]
    att_width = d - conv_width
    m = batch * seq
    assert depth >= 1
    assert seq % Q_TILE == 0 and Q_TILE % CHUNK == 0
    assert (KV_BLOCKS - 1) * Q_TILE == LEFT_CHUNKS * CHUNK
    assert att_width == conv_width
    head_dim = att_width // N_HEADS
    col_scale = jnp.where(jnp.arange(w_in.shape[2]) < att_width,
                          head_dim ** -0.5 * LOG2_E, 1.0).astype(F32)

    h = x.reshape(m, d)
    for i in range(depth):
        last = i == depth - 1
        proj = _in_projection(h, norm_in_g[i], (w_in[i] * col_scale).astype(BF16),
                              tm=min(1024, m), tn=min(1024, d))
        y_a = _attention(proj, _bias_rows(rel_table[i]), attn_out_g[i],
                         batch=batch, seq=seq, width=att_width, n_heads=N_HEADS)
        y_c = _conv_module(proj, w_dw[i], b_dw[i], conv_ln_g[i], conv_ln_b[i],
                           w_pw[i].astype(BF16), b_pw[i], conv_out_g[i],
                           batch=batch, seq=seq, width=conv_width, col0=4, tt=min(256, seq))
        w_pair = jnp.stack([w_out[i], w_ple_gate[i]]).astype(BF16)
        h = _out_block(h, y_a, y_c, w_pair, ple_norm_g[i], b_ple_gate[i], p[i].reshape(m, -1),
                       w_ple[i].astype(BF16), final_g, tm=min(512, m), tn=min(1024, d),
                       apply_final=last)
    return h.reshape(batch, seq, d)
```

```python
import functools

import numpy as np
import jax
import jax.numpy as jnp
from jax import lax
from jax.experimental import pallas as pl
from jax.experimental.pallas import tpu as pltpu

CHUNK = 64
LEFT_CHUNKS = 8
N_HEADS = 16
MAX_REL = 256
CONV_K = 31
EPS = 1e-6
NEG_INF = -1e30
LOG2_E = 1.4426950408889634

LANES = 128
Q_TILE = 256
KV_BLOCKS = 3
BAND_KEYS = KV_BLOCKS * Q_TILE
BIAS_ROW = 1024
HALO = 32
ROW_CHUNK = 32
VMEM_LIMIT = 60 * 1024 * 1024

F32 = jnp.float32
BF16 = jnp.bfloat16


def _cparams(n_axes):
    return pltpu.CompilerParams(
        dimension_semantics=("arbitrary",) * n_axes, vmem_limit_bytes=VMEM_LIMIT)


def _row_loop(n_rows, body, unroll=1):
    def step(r, carry):
        body(pl.ds(pl.multiple_of(r * ROW_CHUNK, ROW_CHUNK), ROW_CHUNK))
        return carry
    lax.fori_loop(0, n_rows // ROW_CHUNK, step, 0, unroll=unroll)


def _silu(z):
    return z * jax.nn.sigmoid(z)


def _proj_kernel(x_hbm, g_ref, w_ref, o_ref, x_buf, xn_ref, sem):
    i = pl.program_id(0)
    j = pl.program_id(1)
    tm = x_buf.shape[0]

    def x_copy(block):
        rows = pl.ds(pl.multiple_of(block * tm, tm), tm)
        return pltpu.make_async_copy(x_hbm.at[rows, :], x_buf, sem)

    @pl.when((i == 0) & (j == 0))
    def _():
        x_copy(0).start()

    @pl.when(j == 0)
    def _():
        x_copy(i).wait()

        def norm_rows(rows):
            x = x_buf[rows, :]
            ms = jnp.mean(x * x, axis=-1, keepdims=True)
            xn_ref[rows, :] = (x * lax.rsqrt(ms + EPS) * g_ref[...]).astype(BF16)
        _row_loop(tm, norm_rows, unroll=2)

    @pl.when((j == 1) & (i + 1 < pl.num_programs(0)))
    def _():
        x_copy(i + 1).start()

    o_ref[...] = jnp.dot(xn_ref[...], w_ref[...], preferred_element_type=F32).astype(o_ref.dtype)


def _in_projection(x2, g, w_bf, *, tm, tn):
    m, d = x2.shape
    n = w_bf.shape[1]
    assert n // tn >= 2
    return pl.pallas_call(
        _proj_kernel,
        out_shape=jax.ShapeDtypeStruct((m, n), BF16),
        grid=(m // tm, n // tn),
        in_specs=[
            pl.BlockSpec(memory_space=pl.ANY),
            pl.BlockSpec((1, d), lambda i, j: (0, 0)),
            pl.BlockSpec((d, tn), lambda i, j: (0, j)),
        ],
        out_specs=pl.BlockSpec((tm, tn), lambda i, j: (i, j)),
        scratch_shapes=[pltpu.VMEM((tm, d), F32), pltpu.VMEM((tm, d), BF16),
                        pltpu.SemaphoreType.DMA(())],
        compiler_params=_cparams(2),
        name="in_projection",
    )(x2, g.reshape(1, d), w_bf)


def _attn_kernel(q_ref, k0_ref, k1_ref, k2_ref, v0_ref, v1_ref, v2_ref, z_ref, brow_ref, g_ref,
                 o_ref, y_ref, bias_ref, vext_ref, *, n_heads, head_dim):
    t = pl.program_id(1)
    k_refs = (k0_ref, k1_ref, k2_ref)
    v_refs = (v0_ref, v1_ref, v2_ref)

    @pl.when((pl.program_id(0) == 0) & (t == 0))
    def _():
        qi = lax.broadcasted_iota(jnp.int32, (Q_TILE, BAND_KEYS), 0) // CHUNK
        kj = lax.broadcasted_iota(jnp.int32, (Q_TILE, BAND_KEYS), 1) // CHUNK
        in_band = (kj >= qi) & (kj <= qi + LEFT_CHUNKS)
        for h in range(n_heads):
            row = jnp.broadcast_to(brow_ref[h:h + 1, :], (Q_TILE, BIAS_ROW))
            skew = pltpu.roll(row, 0, 1, stride=1, stride_axis=0)
            bias_ref[h, :, 0:BAND_KEYS] = jnp.where(in_band, skew[:, :BAND_KEYS], NEG_INF)
            bias_ref[h, :, BAND_KEYS:] = jnp.full((Q_TILE, Q_TILE), NEG_INF, F32)
        vext_ref[:, :, head_dim:] = jnp.ones((n_heads, BAND_KEYS, head_dim), BF16)

    ss = jnp.zeros((Q_TILE, 1), F32)
    for h in range(n_heads):
        cols = slice(h * head_dim, (h + 1) * head_dim)
        q = q_ref[:, cols]
        s = []
        for j in range(KV_BLOCKS):
            vext_ref[h, j * Q_TILE:(j + 1) * Q_TILE, 0:head_dim] = v_refs[j][:, cols]
            sj = lax.dot_general(q, k_refs[j][:, cols], (((1,), (1,)), ((), ())),
                                 preferred_element_type=F32)
            if j < KV_BLOCKS - 1:
                valid = t + (j - (KV_BLOCKS - 1)) >= 0
                off = pl.multiple_of(jnp.where(valid, j * Q_TILE, BAND_KEYS), Q_TILE)
            else:
                off = j * Q_TILE
            s.append(sj + bias_ref[h, :, pl.ds(off, Q_TILE)])
        m = jnp.max(jnp.maximum(jnp.maximum(s[0], s[1]), s[2]), axis=-1, keepdims=True)
        oe = None
        for j in range(KV_BLOCKS):
            pj = jnp.exp2(s[j] - m).astype(BF16)
            part = jnp.dot(pj, vext_ref[h, j * Q_TILE:(j + 1) * Q_TILE, :],
                           preferred_element_type=F32)
            oe = part if oe is None else oe + part
        y = oe[:, 0:head_dim] / oe[:, head_dim:]
        ss = ss + jnp.sum(y * y, axis=-1, keepdims=True)
        y_ref[:, cols] = y * (g_ref[:, cols] * _silu(z_ref[:, cols].astype(F32)))

    rinv = lax.rsqrt(ss * (1.0 / (n_heads * head_dim)) + EPS)
    for r0 in range(0, Q_TILE, ROW_CHUNK):
        rows = slice(r0, r0 + ROW_CHUNK)
        o_ref[rows, :] = (y_ref[rows, :] * rinv[rows, :]).astype(o_ref.dtype)


def _bias_rows(table):
    e = np.arange(BIAS_ROW)
    e = np.where(e >= BAND_KEYS, e - BIAS_ROW, e)
    idx = np.clip((KV_BLOCKS - 1) * Q_TILE - e, -MAX_REL, MAX_REL) + MAX_REL
    return table[:, idx].astype(F32) * LOG2_E


def _attention(proj, brow, g, *, batch, seq, width, n_heads):
    m = proj.shape[0]
    tiles = seq // Q_TILE
    head_dim = width // n_heads

    def spec(off, col):
        return pl.BlockSpec((Q_TILE, width),
                            lambda b, t: (b * tiles + jnp.maximum(t + off, 0), col))

    in_specs = [spec(0, 0),
                spec(-2, 1), spec(-1, 1), spec(0, 1),
                spec(-2, 2), spec(-1, 2), spec(0, 2),
                spec(0, 3),
                pl.BlockSpec(brow.shape, lambda b, t: (0, 0)),
                pl.BlockSpec((1, width), lambda b, t: (0, 0))]
    return pl.pallas_call(
        functools.partial(_attn_kernel, n_heads=n_heads, head_dim=head_dim),
        out_shape=jax.ShapeDtypeStruct((m, width), BF16),
        grid=(batch, tiles),
        in_specs=in_specs,
        out_specs=pl.BlockSpec((Q_TILE, width), lambda b, t: (b * tiles + t, 0)),
        scratch_shapes=[pltpu.VMEM((Q_TILE, width), F32),
                        pltpu.VMEM((n_heads, Q_TILE, BAND_KEYS + Q_TILE), F32),
                        pltpu.VMEM((n_heads, BAND_KEYS, 2 * head_dim), BF16)],
        compiler_params=_cparams(2),
        name="band_attention",
    )(*([proj] * 8), brow, g.reshape(1, width))


def _conv_kernel(a_ref, gl_ref, z_ref, wdw_ref, bdw_ref, lng_ref, lnb_ref, wpw_ref, bpw_ref, g_ref,
                 o_ref, u2_ref, c_ref, yb_ref, pw_ref, *, strip_rows):
    tt, width = a_ref.shape
    n_slabs = width // LANES
    t = pl.program_id(1)

    @pl.when(t == 0)
    def _():
        u2_ref[...] = jnp.zeros_like(u2_ref)

    @pl.when(t > 0)
    def _():
        u2_ref[:, 0:2 * HALO, :] = u2_ref[:, 2 * tt:2 * (tt + HALO), :]

    def glu_rows(rows):
        u = a_ref[rows, :].astype(F32) * jax.nn.sigmoid(gl_ref[rows, :].astype(F32))
        for c in range(n_slabs):
            u2_ref[c, pl.ds(2 * (HALO + rows.start), ROW_CHUNK, stride=2), :] = (
                u[:, c * LANES:(c + 1) * LANES])
    _row_loop(tt, glu_rows)

    def conv_slab(c, carry):
        cols = pl.ds(pl.multiple_of(c * LANES, LANES), LANES)
        for r0 in range(0, tt, strip_rows):
            acc = jnp.zeros((strip_rows, LANES), F32)
            for k in range(CONV_K):
                first = r0 + HALO - (CONV_K - 1) + k
                acc += u2_ref[c, pl.ds(2 * first, strip_rows, stride=2), :] * wdw_ref[k:k + 1, cols]
            c_ref[r0:r0 + strip_rows, cols] = acc + bdw_ref[:, cols]
        return carry
    lax.fori_loop(0, n_slabs, conv_slab, 0)

    def norm_rows(rows):
        c = c_ref[rows, :]
        mu = jnp.mean(c, axis=-1, keepdims=True)
        cc = c - mu
        var = jnp.mean(cc * cc, axis=-1, keepdims=True)
        y = cc * lax.rsqrt(var + EPS) * lng_ref[...] + lnb_ref[...]
        yb_ref[rows, :] = _silu(y).astype(BF16)
    _row_loop(tt, norm_rows, unroll=2)

    pw_ref[...] = jnp.dot(yb_ref[...], wpw_ref[...], preferred_element_type=F32)

    def finish_rows(rows):
        pw = pw_ref[rows, :] + bpw_ref[...]
        ms = jnp.mean(pw * pw, axis=-1, keepdims=True)
        z = z_ref[rows, :].astype(F32)
        o_ref[rows, :] = (pw * lax.rsqrt(ms + EPS) * g_ref[...] * _silu(z)).astype(o_ref.dtype)
    _row_loop(tt, finish_rows, unroll=2)


def _conv_module(proj, w_dw, b_dw, ln_g, ln_b, w_pw_bf, b_pw, g, *, batch, seq, width, col0, tt):
    m = proj.shape[0]
    tiles = seq // tt
    w_dw_p = jnp.zeros((HALO, width), F32).at[:CONV_K].set(w_dw)

    def spec(col):
        return pl.BlockSpec((tt, width), lambda b, t: (b * tiles + t, col))

    def vec():
        return pl.BlockSpec((1, width), lambda b, t: (0, 0))

    return pl.pallas_call(
        functools.partial(_conv_kernel, strip_rows=128),
        out_shape=jax.ShapeDtypeStruct((m, width), BF16),
        grid=(batch, tiles),
        in_specs=[spec(col0), spec(col0 + 1), spec(col0 + 2),
                  pl.BlockSpec((HALO, width), lambda b, t: (0, 0)),
                  vec(), vec(), vec(),
                  pl.BlockSpec((width, width), lambda b, t: (0, 0)),
                  vec(), vec()],
        out_specs=pl.BlockSpec((tt, width), lambda b, t: (b * tiles + t, 0)),
        scratch_shapes=[pltpu.VMEM((width // LANES, 2 * (tt + HALO), LANES), F32),
                        pltpu.VMEM((tt, width), F32),
                        pltpu.VMEM((tt, width), BF16),
                        pltpu.VMEM((tt, width), F32)],
        compiler_params=_cparams(2),
        name="conv_module",
    )(proj, proj, proj, w_dw_p, b_dw.reshape(1, width), ln_g.reshape(1, width),
      ln_b.reshape(1, width), w_pw_bf, b_pw.reshape(1, width), g.reshape(1, width))


def _cast_pair_kernel(a_ref, b_ref, o_ref):
    @pl.when(pl.program_id(0) == 0)
    def _():
        o_ref[...] = a_ref[...].astype(o_ref.dtype)

    @pl.when(pl.program_id(0) == 1)
    def _():
        o_ref[...] = b_ref[...].astype(o_ref.dtype)


def _cast_pair(a, b, *, rows):
    r, c = a.shape
    nr = r // rows
    return pl.pallas_call(
        _cast_pair_kernel,
        out_shape=jax.ShapeDtypeStruct((2, r, c), BF16),
        grid=(2, nr),
        in_specs=[pl.BlockSpec((rows, c), lambda s, t: (jnp.where(s == 0, t, nr - 1), 0)),
                  pl.BlockSpec((rows, c), lambda s, t: (jnp.where(s == 1, t, 0), 0))],
        out_specs=pl.BlockSpec((None, rows, c), lambda s, t: (s, t, 0)),
        compiler_params=_cparams(2),
        name="cast_weight_pair",
    )(a, b)


def _out_kernel(x_ref, ya_ref, yc_ref, w_ref, pg_ref, bg_ref, p_ref, wp_ref, fg_ref,
                o_ref, h_ref, hn_ref, ssh_ref, sso_ref, *, nj, tn, n_blocks, apply_final):
    i = pl.program_id(0)
    j = pl.program_id(1)
    cur = i % 2
    prev = 1 - cur
    wa = ya_ref.shape[1]
    d = hn_ref.shape[1]
    to = o_ref.shape[1]

    @pl.when((i < n_blocks) & (j == 0))
    def _():
        ssh_ref[...] = jnp.zeros_like(ssh_ref)
        sso_ref[cur] = jnp.zeros(sso_ref.shape[1:], F32)

    @pl.when((i < n_blocks) & (j < nj))
    def _():
        cols = pl.ds(pl.multiple_of(j * tn, tn), tn)
        acc = jnp.dot(ya_ref[...], w_ref[0:wa, :], preferred_element_type=F32)
        acc += jnp.dot(yc_ref[...], w_ref[wa:, :], preferred_element_type=F32)
        h = x_ref[...] + acc
        h_ref[cur, :, cols] = h
        hn_ref[:, cols] = (h * pg_ref[:, cols]).astype(BF16)
        ssh_ref[...] += jnp.sum(h * h, axis=-1, keepdims=True)

    @pl.when((i < n_blocks) & (j >= nj))
    def _():
        cols = pl.ds(pl.multiple_of((j - nj) * tn, tn), tn)
        rinv = lax.rsqrt(ssh_ref[...] * (1.0 / d) + EPS)
        gate = jax.nn.sigmoid(
            jnp.dot(hn_ref[...], w_ref[...], preferred_element_type=F32) * rinv + bg_ref[...])
        pe = jnp.dot(p_ref[...].astype(BF16), wp_ref[...], preferred_element_type=F32)
        h2 = h_ref[cur, :, cols] + gate * pe
        h_ref[cur, :, cols] = h2
        sso_ref[cur] += jnp.sum(h2 * h2, axis=-1, keepdims=True)

    @pl.when(i > 0)
    def _():
        ocols = pl.ds(pl.multiple_of(j * to, to), to)
        y = h_ref[prev, :, ocols]
        if apply_final:
            y = y * lax.rsqrt(sso_ref[prev] * (1.0 / d) + EPS) * fg_ref[:, ocols]
        o_ref[...] = y


def _out_block(x2, y_a, y_c, w_pair_bf, ple_g, b_gate, p2, w_ple_bf, final_g, *, tm, tn, apply_final):
    m, d = x2.shape
    wa = y_a.shape[1]
    wc = y_c.shape[1]
    pd = p2.shape[1]
    nj = d // tn
    nb = m // tm
    to = d // (2 * nj)

    def row(i):
        return jnp.minimum(i, nb - 1)

    def first(j):
        return jnp.minimum(j, nj - 1)

    def second(j):
        return jnp.maximum(j - nj, 0)

    def w_block(i, j):
        jj = jnp.where(i < nb, j, 2 * nj - 1)
        return (jj // nj, 0, jj % nj)

    return pl.pallas_call(
        functools.partial(_out_kernel, nj=nj, tn=tn, n_blocks=nb, apply_final=apply_final),
        out_shape=jax.ShapeDtypeStruct((m, d), F32),
        grid=(nb + 1, 2 * nj),
        in_specs=[
            pl.BlockSpec((tm, tn), lambda i, j: (row(i), first(j))),
            pl.BlockSpec((tm, wa), lambda i, j: (row(i), 0)),
            pl.BlockSpec((tm, wc), lambda i, j: (row(i), 0)),
            pl.BlockSpec((None, d, tn), w_block),
            pl.BlockSpec((1, d), lambda i, j: (0, 0)),
            pl.BlockSpec((1, tn), lambda i, j: (0, second(j))),
            pl.BlockSpec((tm, pd), lambda i, j: (row(i), 0)),
            pl.BlockSpec((pd, tn), lambda i, j: (0, second(j))),
            pl.BlockSpec((1, d), lambda i, j: (0, 0)),
        ],
        out_specs=pl.BlockSpec((tm, to),
                               lambda i, j: (jnp.maximum(i - 1, 0), jnp.where(i == 0, 0, j))),
        scratch_shapes=[pltpu.VMEM((2, tm, d), F32), pltpu.VMEM((tm, d), BF16),
                        pltpu.VMEM((tm, 1), F32), pltpu.VMEM((2, tm, 1), F32)],
        compiler_params=_cparams(2),
        name="out_block",
    )(x2, y_a, y_c, w_pair_bf, ple_g.reshape(1, d), b_gate.reshape(1, d), p2, w_ple_bf,
      final_g.reshape(1, d))


def kernel(x, p, norm_in_g, w_in, rel_table, w_dw, b_dw, conv_ln_g, conv_ln_b, w_pw, b_pw,
           attn_out_g, conv_out_g, w_out, ple_norm_g, w_ple_gate, b_ple_gate, w_ple, final_g):
    batch, seq, d = x.shape
    depth = w_in.shape[0]
    conv_width = w_pw.shape[1]
    att_width = d - conv_width
    m = batch * seq
    assert depth >= 1
    assert seq % Q_TILE == 0 and Q_TILE % CHUNK == 0
    assert (KV_BLOCKS - 1) * Q_TILE == LEFT_CHUNKS * CHUNK
    assert att_width == conv_width
    head_dim = att_width // N_HEADS
    col_scale = jnp.where(jnp.arange(w_in.shape[2]) < att_width,
                          head_dim ** -0.5 * LOG2_E, 1.0).astype(F32)

    h = x.reshape(m, d)
    for i in range(depth):
        last = i == depth - 1
        proj = _in_projection(h, norm_in_g[i], (w_in[i] * col_scale).astype(BF16),
                              tm=min(1024, m), tn=min(1024, d))
        y_a = _attention(proj, _bias_rows(rel_table[i]), attn_out_g[i],
                         batch=batch, seq=seq, width=att_width, n_heads=N_HEADS)
        y_c = _conv_module(proj, w_dw[i], b_dw[i], conv_ln_g[i], conv_ln_b[i],
                           w_pw[i].astype(BF16), b_pw[i], conv_out_g[i],
                           batch=batch, seq=seq, width=conv_width, col0=4, tt=min(256, seq))
        w_pair = _cast_pair(w_out[i], w_ple_gate[i], rows=256)
        h = _out_block(h, y_a, y_c, w_pair, ple_norm_g[i], b_ple_gate[i], p[i].reshape(m, -1),
                       w_ple[i].astype(BF16), final_g, tm=min(512, m), tn=min(1024, d),
                       apply_final=last)
    return h.reshape(batch, seq, d)
```

```python
import functools

import numpy as np
import jax
import jax.numpy as jnp
from jax import lax
from jax.experimental import pallas as pl
from jax.experimental.pallas import tpu as pltpu

CHUNK = 64
LEFT_CHUNKS = 8
N_HEADS = 16
MAX_REL = 256
CONV_K = 31
EPS = 1e-6
NEG_INF = -1e30
LOG2_E = 1.4426950408889634

LANES = 128
Q_TILE = 256
KV_BLOCKS = 3
BAND_KEYS = KV_BLOCKS * Q_TILE
BIAS_ROW = 1024
HALO = 32
ROW_CHUNK = 32
VMEM_LIMIT = 60 * 1024 * 1024

F32 = jnp.float32
BF16 = jnp.bfloat16


def _cparams(n_axes):
    return pltpu.CompilerParams(
        dimension_semantics=("arbitrary",) * n_axes, vmem_limit_bytes=VMEM_LIMIT)


def _row_loop(n_rows, body, unroll=1):
    def step(r, carry):
        body(pl.ds(pl.multiple_of(r * ROW_CHUNK, ROW_CHUNK), ROW_CHUNK))
        return carry
    lax.fori_loop(0, n_rows // ROW_CHUNK, step, 0, unroll=unroll)


def _sigmoid(z):
    return 0.5 * jnp.tanh(0.5 * z) + 0.5


def _silu(z):
    return z * _sigmoid(z)


def _proj_kernel(x_hbm, g_ref, w_ref, o_ref, x_buf, xn_ref, rinv_ref, sem):
    i = pl.program_id(0)
    j = pl.program_id(1)
    tm = x_buf.shape[0]

    def x_copy(block):
        rows = pl.ds(pl.multiple_of(block * tm, tm), tm)
        return pltpu.make_async_copy(x_hbm.at[rows, :], x_buf, sem)

    @pl.when((i == 0) & (j == 0))
    def _():
        x_copy(0).start()

    @pl.when(j == 0)
    def _():
        x_copy(i).wait()

        def norm_rows(rows):
            x = x_buf[rows, :]
            rinv_ref[rows, :] = lax.rsqrt(jnp.mean(x * x, axis=-1, keepdims=True) + EPS)
            xn_ref[rows, :] = (x * g_ref[...]).astype(BF16)
        _row_loop(tm, norm_rows, unroll=2)

    @pl.when((j == 1) & (i + 1 < pl.num_programs(0)))
    def _():
        x_copy(i + 1).start()

    acc = jnp.dot(xn_ref[...], w_ref[...], preferred_element_type=F32)
    o_ref[...] = (acc * rinv_ref[...]).astype(o_ref.dtype)


def _in_projection(x2, g, w_bf, *, tm, tn):
    m, d = x2.shape
    n = w_bf.shape[1]
    assert n // tn >= 2
    return pl.pallas_call(
        _proj_kernel,
        out_shape=jax.ShapeDtypeStruct((m, n), BF16),
        grid=(m // tm, n // tn),
        in_specs=[
            pl.BlockSpec(memory_space=pl.ANY),
            pl.BlockSpec((1, d), lambda i, j: (0, 0)),
            pl.BlockSpec((d, tn), lambda i, j: (0, j)),
        ],
        out_specs=pl.BlockSpec((tm, tn), lambda i, j: (i, j)),
        scratch_shapes=[pltpu.VMEM((tm, d), F32), pltpu.VMEM((tm, d), BF16),
                        pltpu.VMEM((tm, 1), F32), pltpu.SemaphoreType.DMA(())],
        compiler_params=_cparams(2),
        name="in_projection",
    )(x2, g.reshape(1, d), w_bf)


def _attn_kernel(q_ref, k0_ref, k1_ref, k2_ref, v0_ref, v1_ref, v2_ref, z_ref, brow_ref, g_ref,
                 o_ref, y_ref, bias_ref, vext_ref, *, n_heads, head_dim):
    t = pl.program_id(1)
    k_refs = (k0_ref, k1_ref, k2_ref)
    v_refs = (v0_ref, v1_ref, v2_ref)

    @pl.when((pl.program_id(0) == 0) & (t == 0))
    def _():
        qi = lax.broadcasted_iota(jnp.int32, (Q_TILE, BAND_KEYS), 0) // CHUNK
        kj = lax.broadcasted_iota(jnp.int32, (Q_TILE, BAND_KEYS), 1) // CHUNK
        in_band = (kj >= qi) & (kj <= qi + LEFT_CHUNKS)
        for h in range(n_heads):
            row = jnp.broadcast_to(brow_ref[h:h + 1, :], (Q_TILE, BIAS_ROW))
            skew = pltpu.roll(row, 0, 1, stride=1, stride_axis=0)
            bias_ref[h, :, 0:BAND_KEYS] = jnp.where(in_band, skew[:, :BAND_KEYS], NEG_INF)
            bias_ref[h, :, BAND_KEYS:] = jnp.full((Q_TILE, Q_TILE), NEG_INF, F32)
        vext_ref[:, :, head_dim:] = jnp.ones((n_heads, BAND_KEYS, head_dim), BF16)

    ss = jnp.zeros((Q_TILE, 1), F32)
    for h in range(n_heads):
        cols = slice(h * head_dim, (h + 1) * head_dim)
        q = q_ref[:, cols]
        s = []
        for j in range(KV_BLOCKS):
            vext_ref[h, j * Q_TILE:(j + 1) * Q_TILE, 0:head_dim] = v_refs[j][:, cols]
            sj = lax.dot_general(q, k_refs[j][:, cols], (((1,), (1,)), ((), ())),
                                 preferred_element_type=F32)
            if j < KV_BLOCKS - 1:
                valid = t + (j - (KV_BLOCKS - 1)) >= 0
                off = pl.multiple_of(jnp.where(valid, j * Q_TILE, BAND_KEYS), Q_TILE)
            else:
                off = j * Q_TILE
            s.append(sj + bias_ref[h, :, pl.ds(off, Q_TILE)])
        m = jnp.max(jnp.maximum(jnp.maximum(s[0], s[1]), s[2]), axis=-1, keepdims=True)
        oe = None
        for j in range(KV_BLOCKS):
            pj = jnp.exp2(s[j] - m).astype(BF16)
            part = jnp.dot(pj, vext_ref[h, j * Q_TILE:(j + 1) * Q_TILE, :],
                           preferred_element_type=F32)
            oe = part if oe is None else oe + part
        y = oe[:, 0:head_dim] / oe[:, head_dim:]
        ss = ss + jnp.sum(y * y, axis=-1, keepdims=True)
        y_ref[:, cols] = y * (g_ref[:, cols] * _silu(z_ref[:, cols].astype(F32)))

    rinv = lax.rsqrt(ss * (1.0 / (n_heads * head_dim)) + EPS)
    for r0 in range(0, Q_TILE, ROW_CHUNK):
        rows = slice(r0, r0 + ROW_CHUNK)
        o_ref[rows, :] = (y_ref[rows, :] * rinv[rows, :]).astype(o_ref.dtype)


def _bias_rows(table):
    e = np.arange(BIAS_ROW)
    e = np.where(e >= BAND_KEYS, e - BIAS_ROW, e)
    idx = np.clip((KV_BLOCKS - 1) * Q_TILE - e, -MAX_REL, MAX_REL) + MAX_REL
    return table[:, idx].astype(F32) * LOG2_E


def _attention(proj, brow, g, *, batch, seq, width, n_heads):
    m = proj.shape[0]
    tiles = seq // Q_TILE
    head_dim = width // n_heads

    def spec(off, col):
        return pl.BlockSpec((Q_TILE, width),
                            lambda b, t: (b * tiles + jnp.maximum(t + off, 0), col))

    in_specs = [spec(0, 0),
                spec(-2, 1), spec(-1, 1), spec(0, 1),
                spec(-2, 2), spec(-1, 2), spec(0, 2),
                spec(0, 3),
                pl.BlockSpec(brow.shape, lambda b, t: (0, 0)),
                pl.BlockSpec((1, width), lambda b, t: (0, 0))]
    return pl.pallas_call(
        functools.partial(_attn_kernel, n_heads=n_heads, head_dim=head_dim),
        out_shape=jax.ShapeDtypeStruct((m, width), BF16),
        grid=(batch, tiles),
        in_specs=in_specs,
        out_specs=pl.BlockSpec((Q_TILE, width), lambda b, t: (b * tiles + t, 0)),
        scratch_shapes=[pltpu.VMEM((Q_TILE, width), F32),
                        pltpu.VMEM((n_heads, Q_TILE, BAND_KEYS + Q_TILE), F32),
                        pltpu.VMEM((n_heads, BAND_KEYS, 2 * head_dim), BF16)],
        compiler_params=_cparams(2),
        name="band_attention",
    )(*([proj] * 8), brow, g.reshape(1, width))


def _conv_kernel(a_ref, gl_ref, z_ref, wdw_ref, bdw_ref, lng_ref, lnb_ref, wpw_ref, bpw_ref, g_ref,
                 o_ref, u2_ref, c_ref, yb_ref, pw_ref, *, strip_rows):
    tt, width = a_ref.shape
    n_slabs = width // LANES
    t = pl.program_id(1)

    @pl.when(t == 0)
    def _():
        u2_ref[...] = jnp.zeros_like(u2_ref)

    @pl.when(t > 0)
    def _():
        u2_ref[:, 0:2 * HALO, :] = u2_ref[:, 2 * tt:2 * (tt + HALO), :]

    def glu_rows(rows):
        u = a_ref[rows, :].astype(F32) * _sigmoid(gl_ref[rows, :].astype(F32))
        for c in range(n_slabs):
            u2_ref[c, pl.ds(2 * (HALO + rows.start), ROW_CHUNK, stride=2), :] = (
                u[:, c * LANES:(c + 1) * LANES])
    _row_loop(tt, glu_rows)

    def conv_slab(c, carry):
        cols = pl.ds(pl.multiple_of(c * LANES, LANES), LANES)
        for r0 in range(0, tt, strip_rows):
            acc = jnp.zeros((strip_rows, LANES), F32)
            for k in range(CONV_K):
                first = r0 + HALO - (CONV_K - 1) + k
                acc += u2_ref[c, pl.ds(2 * first, strip_rows, stride=2), :] * wdw_ref[k:k + 1, cols]
            c_ref[r0:r0 + strip_rows, cols] = acc + bdw_ref[:, cols]
        return carry
    lax.fori_loop(0, n_slabs, conv_slab, 0)

    def norm_rows(rows):
        c = c_ref[rows, :]
        mu = jnp.mean(c, axis=-1, keepdims=True)
        cc = c - mu
        var = jnp.mean(cc * cc, axis=-1, keepdims=True)
        y = cc * lax.rsqrt(var + EPS) * lng_ref[...] + lnb_ref[...]
        yb_ref[rows, :] = _silu(y).astype(BF16)
    _row_loop(tt, norm_rows, unroll=4)

    pw_ref[...] = jnp.dot(yb_ref[...], wpw_ref[...], preferred_element_type=F32)

    def finish_rows(rows):
        pw = pw_ref[rows, :] + bpw_ref[...]
        ms = jnp.mean(pw * pw, axis=-1, keepdims=True)
        z = z_ref[rows, :].astype(F32)
        o_ref[rows, :] = (pw * lax.rsqrt(ms + EPS) * g_ref[...] * _silu(z)).astype(o_ref.dtype)
    _row_loop(tt, finish_rows, unroll=2)


def _conv_module(proj, w_dw, b_dw, ln_g, ln_b, w_pw_bf, b_pw, g, *, batch, seq, width, col0, tt):
    m = proj.shape[0]
    tiles = seq // tt
    w_dw_p = jnp.zeros((HALO, width), F32).at[:CONV_K].set(w_dw)

    def spec(col):
        return pl.BlockSpec((tt, width), lambda b, t: (b * tiles + t, col))

    def vec():
        return pl.BlockSpec((1, width), lambda b, t: (0, 0))

    return pl.pallas_call(
        functools.partial(_conv_kernel, strip_rows=128),
        out_shape=jax.ShapeDtypeStruct((m, width), BF16),
        grid=(batch, tiles),
        in_specs=[spec(col0), spec(col0 + 1), spec(col0 + 2),
                  pl.BlockSpec((HALO, width), lambda b, t: (0, 0)),
                  vec(), vec(), vec(),
                  pl.BlockSpec((width, width), lambda b, t: (0, 0)),
                  vec(), vec()],
        out_specs=pl.BlockSpec((tt, width), lambda b, t: (b * tiles + t, 0)),
        scratch_shapes=[pltpu.VMEM((width // LANES, 2 * (tt + HALO), LANES), F32),
                        pltpu.VMEM((tt, width), F32),
                        pltpu.VMEM((tt, width), BF16),
                        pltpu.VMEM((tt, width), F32)],
        compiler_params=_cparams(2),
        name="conv_module",
    )(proj, proj, proj, w_dw_p, b_dw.reshape(1, width), ln_g.reshape(1, width),
      ln_b.reshape(1, width), w_pw_bf, b_pw.reshape(1, width), g.reshape(1, width))


def _cast_pair_kernel(a_ref, b_ref, o_ref):
    @pl.when(pl.program_id(0) == 0)
    def _():
        o_ref[...] = a_ref[...].astype(o_ref.dtype)

    @pl.when(pl.program_id(0) == 1)
    def _():
        o_ref[...] = b_ref[...].astype(o_ref.dtype)


def _cast_pair(a, b, *, rows):
    r, c = a.shape
    nr = r // rows
    return pl.pallas_call(
        _cast_pair_kernel,
        out_shape=jax.ShapeDtypeStruct((2, r, c), BF16),
        grid=(2, nr),
        in_specs=[pl.BlockSpec((rows, c), lambda s, t: (jnp.where(s == 0, t, nr - 1), 0)),
                  pl.BlockSpec((rows, c), lambda s, t: (jnp.where(s == 1, t, 0), 0))],
        out_specs=pl.BlockSpec((None, rows, c), lambda s, t: (s, t, 0)),
        compiler_params=_cparams(2),
        name="cast_weight_pair",
    )(a, b)


def _out_kernel(x_ref, ya_ref, yc_ref, w_ref, pg_ref, bg_ref, p_ref, wp_ref, fg_ref,
                o_ref, h_ref, hn_ref, ssh_ref, sso_ref, *, nj, tn, n_blocks, apply_final):
    i = pl.program_id(0)
    j = pl.program_id(1)
    cur = i % 2
    prev = 1 - cur
    wa = ya_ref.shape[1]
    d = hn_ref.shape[1]
    to = o_ref.shape[1]

    @pl.when((i < n_blocks) & (j == 0))
    def _():
        ssh_ref[...] = jnp.zeros_like(ssh_ref)
        sso_ref[cur] = jnp.zeros(sso_ref.shape[1:], F32)

    @pl.when((i < n_blocks) & (j < nj))
    def _():
        cols = pl.ds(pl.multiple_of(j * tn, tn), tn)
        acc = jnp.dot(ya_ref[...], w_ref[0:wa, :], preferred_element_type=F32)
        acc += jnp.dot(yc_ref[...], w_ref[wa:, :], preferred_element_type=F32)
        h = x_ref[...] + acc
        h_ref[cur, :, cols] = h
        hn_ref[:, cols] = (h * pg_ref[:, cols]).astype(BF16)
        ssh_ref[...] += jnp.sum(h * h, axis=-1, keepdims=True)

    @pl.when((i < n_blocks) & (j >= nj))
    def _():
        cols = pl.ds(pl.multiple_of((j - nj) * tn, tn), tn)
        rinv = lax.rsqrt(ssh_ref[...] * (1.0 / d) + EPS)
        gate = _sigmoid(
            jnp.dot(hn_ref[...], w_ref[...], preferred_element_type=F32) * rinv + bg_ref[...])
        pe = jnp.dot(p_ref[...].astype(BF16), wp_ref[...], preferred_element_type=F32)
        h2 = h_ref[cur, :, cols] + gate * pe
        h_ref[cur, :, cols] = h2
        sso_ref[cur] += jnp.sum(h2 * h2, axis=-1, keepdims=True)

    @pl.when(i > 0)
    def _():
        ocols = pl.ds(pl.multiple_of(j * to, to), to)
        y = h_ref[prev, :, ocols]
        if apply_final:
            y = y * lax.rsqrt(sso_ref[prev] * (1.0 / d) + EPS) * fg_ref[:, ocols]
        o_ref[...] = y


def _out_block(x2, y_a, y_c, w_pair_bf, ple_g, b_gate, p2, w_ple_bf, final_g, *, tm, tn, apply_final):
    m, d = x2.shape
    wa = y_a.shape[1]
    wc = y_c.shape[1]
    pd = p2.shape[1]
    nj = d // tn
    nb = m // tm
    to = d // (2 * nj)

    def row(i):
        return jnp.minimum(i, nb - 1)

    def first(j):
        return jnp.minimum(j, nj - 1)

    def second(j):
        return jnp.maximum(j - nj, 0)

    def w_block(i, j):
        jj = jnp.where(i < nb, j, 2 * nj - 1)
        return (jj // nj, 0, jj % nj)

    return pl.pallas_call(
        functools.partial(_out_kernel, nj=nj, tn=tn, n_blocks=nb, apply_final=apply_final),
        out_shape=jax.ShapeDtypeStruct((m, d), F32),
        grid=(nb + 1, 2 * nj),
        in_specs=[
            pl.BlockSpec((tm, tn), lambda i, j: (row(i), first(j))),
            pl.BlockSpec((tm, wa), lambda i, j: (row(i), 0)),
            pl.BlockSpec((tm, wc), lambda i, j: (row(i), 0)),
            pl.BlockSpec((None, d, tn), w_block),
            pl.BlockSpec((1, d), lambda i, j: (0, 0)),
            pl.BlockSpec((1, tn), lambda i, j: (0, second(j))),
            pl.BlockSpec((tm, pd), lambda i, j: (row(i), 0)),
            pl.BlockSpec((pd, tn), lambda i, j: (0, second(j))),
            pl.BlockSpec((1, d), lambda i, j: (0, 0)),
        ],
        out_specs=pl.BlockSpec((tm, to),
                               lambda i, j: (jnp.maximum(i - 1, 0), jnp.where(i == 0, 0, j))),
        scratch_shapes=[pltpu.VMEM((2, tm, d), F32), pltpu.VMEM((tm, d), BF16),
                        pltpu.VMEM((tm, 1), F32), pltpu.VMEM((2, tm, 1), F32)],
        compiler_params=_cparams(2),
        name="out_block",
    )(x2, y_a, y_c, w_pair_bf, ple_g.reshape(1, d), b_gate.reshape(1, d), p2, w_ple_bf,
      final_g.reshape(1, d))


def kernel(x, p, norm_in_g, w_in, rel_table, w_dw, b_dw, conv_ln_g, conv_ln_b, w_pw, b_pw,
           attn_out_g, conv_out_g, w_out, ple_norm_g, w_ple_gate, b_ple_gate, w_ple, final_g):
    batch, seq, d = x.shape
    depth = w_in.shape[0]
    conv_width = w_pw.shape[1]
    att_width = d - conv_width
    m = batch * seq
    assert depth >= 1
    assert seq % Q_TILE == 0 and Q_TILE % CHUNK == 0
    assert (KV_BLOCKS - 1) * Q_TILE == LEFT_CHUNKS * CHUNK
    assert att_width == conv_width
    head_dim = att_width // N_HEADS
    col_scale = jnp.where(jnp.arange(w_in.shape[2]) < att_width,
                          head_dim ** -0.5 * LOG2_E, 1.0).astype(F32)

    h = x.reshape(m, d)
    for i in range(depth):
        last = i == depth - 1
        proj = _in_projection(h, norm_in_g[i], (w_in[i] * col_scale).astype(BF16),
                              tm=min(1024, m), tn=min(1024, d))
        y_a = _attention(proj, _bias_rows(rel_table[i]), attn_out_g[i],
                         batch=batch, seq=seq, width=att_width, n_heads=N_HEADS)
        y_c = _conv_module(proj, w_dw[i], b_dw[i], conv_ln_g[i], conv_ln_b[i],
                           w_pw[i].astype(BF16), b_pw[i], conv_out_g[i],
                           batch=batch, seq=seq, width=conv_width, col0=4, tt=min(256, seq))
        w_pair = _cast_pair(w_out[i], w_ple_gate[i], rows=256)
        h = _out_block(h, y_a, y_c, w_pair, ple_norm_g[i], b_ple_gate[i], p[i].reshape(m, -1),
                       w_ple[i].astype(BF16), final_g, tm=min(512, m), tn=min(1024, d),
                       apply_final=last)
    return h.reshape(batch, seq, d)
```

```python
import functools

import numpy as np
import jax
import jax.numpy as jnp
from jax import lax
from jax.experimental import pallas as pl
from jax.experimental.pallas import tpu as pltpu

CHUNK = 64
LEFT_CHUNKS = 8
N_HEADS = 16
MAX_REL = 256
CONV_K = 31
EPS = 1e-6
NEG_INF = -1e30
LOG2_E = 1.4426950408889634

LANES = 128
Q_TILE = 256
KV_BLOCKS = 3
BAND_KEYS = KV_BLOCKS * Q_TILE
BIAS_ROW = 1024
HALO = 32
ROW_CHUNK = 32

V7X_VMEM_BYTES = 64 * 1024 * 1024
VMEM_LIMIT = V7X_VMEM_BYTES - 4 * 1024 * 1024

PROJ_TILE = (1024, 1024)
OUT_TILE = (512, 1024)
CONV_ROWS = 256
CAST_ROWS = 256

F32 = jnp.float32
BF16 = jnp.bfloat16


def _cparams(n_axes):
    return pltpu.CompilerParams(
        dimension_semantics=("arbitrary",) * n_axes, vmem_limit_bytes=VMEM_LIMIT)


def _row_loop(n_rows, body, unroll=1):
    def step(r, carry):
        body(pl.ds(pl.multiple_of(r * ROW_CHUNK, ROW_CHUNK), ROW_CHUNK))
        return carry
    lax.fori_loop(0, n_rows // ROW_CHUNK, step, 0, unroll=unroll)


def _sigmoid(z):
    return 0.5 * jnp.tanh(0.5 * z) + 0.5


def _silu(z):
    return z * _sigmoid(z)


def _proj_kernel(x_hbm, g_ref, w_ref, o_ref, x_buf, xn_ref, sem):
    i = pl.program_id(0)
    j = pl.program_id(1)
    tm = x_buf.shape[0]

    def x_copy(block):
        rows = pl.ds(pl.multiple_of(block * tm, tm), tm)
        return pltpu.make_async_copy(x_hbm.at[rows, :], x_buf, sem)

    @pl.when((i == 0) & (j == 0))
    def _():
        x_copy(0).start()

    @pl.when(j == 0)
    def _():
        x_copy(i).wait()

        def norm_rows(rows):
            x = x_buf[rows, :]
            ms = jnp.mean(x * x, axis=-1, keepdims=True)
            xn_ref[rows, :] = (x * lax.rsqrt(ms + EPS) * g_ref[...]).astype(BF16)
        _row_loop(tm, norm_rows, unroll=2)

    @pl.when((j == 1) & (i + 1 < pl.num_programs(0)))
    def _():
        x_copy(i + 1).start()

    o_ref[...] = jnp.dot(xn_ref[...], w_ref[...], preferred_element_type=F32).astype(o_ref.dtype)


def _in_projection(x2, g, w_bf, *, tm, tn):
    m, d = x2.shape
    n = w_bf.shape[1]
    assert n // tn >= 2
    return pl.pallas_call(
        _proj_kernel,
        out_shape=jax.ShapeDtypeStruct((m, n), BF16),
        grid=(m // tm, n // tn),
        in_specs=[
            pl.BlockSpec(memory_space=pl.ANY),
            pl.BlockSpec((1, d), lambda i, j: (0, 0)),
            pl.BlockSpec((d, tn), lambda i, j: (0, j)),
        ],
        out_specs=pl.BlockSpec((tm, tn), lambda i, j: (i, j)),
        scratch_shapes=[pltpu.VMEM((tm, d), F32), pltpu.VMEM((tm, d), BF16),
                        pltpu.SemaphoreType.DMA(())],
        compiler_params=_cparams(2),
        name="in_projection",
    )(x2, g.reshape(1, d), w_bf)


def _attn_kernel(q_ref, k0_ref, k1_ref, k2_ref, v0_ref, v1_ref, v2_ref, z_ref, brow_ref, g_ref,
                 o_ref, y_ref, bias_ref, vext_ref, *, n_heads, head_dim):
    t = pl.program_id(1)
    k_refs = (k0_ref, k1_ref, k2_ref)
    v_refs = (v0_ref, v1_ref, v2_ref)

    @pl.when((pl.program_id(0) == 0) & (t == 0))
    def _():
        qi = lax.broadcasted_iota(jnp.int32, (Q_TILE, BAND_KEYS), 0) // CHUNK
        kj = lax.broadcasted_iota(jnp.int32, (Q_TILE, BAND_KEYS), 1) // CHUNK
        in_band = (kj >= qi) & (kj <= qi + LEFT_CHUNKS)
        for h in range(n_heads):
            row = jnp.broadcast_to(brow_ref[h:h + 1, :], (Q_TILE, BIAS_ROW))
            skew = pltpu.roll(row, 0, 1, stride=1, stride_axis=0)
            bias_ref[h, :, 0:BAND_KEYS] = jnp.where(in_band, skew[:, :BAND_KEYS], NEG_INF)
            bias_ref[h, :, BAND_KEYS:] = jnp.full((Q_TILE, Q_TILE), NEG_INF, F32)
        vext_ref[:, :, head_dim:] = jnp.ones((n_heads, BAND_KEYS, head_dim), BF16)

    ss = jnp.zeros((Q_TILE, 1), F32)
    for h in range(n_heads):
        cols = slice(h * head_dim, (h + 1) * head_dim)
        q = q_ref[:, cols]
        s = []
        for j in range(KV_BLOCKS):
            vext_ref[h, j * Q_TILE:(j + 1) * Q_TILE, 0:head_dim] = v_refs[j][:, cols]
            sj = lax.dot_general(q, k_refs[j][:, cols], (((1,), (1,)), ((), ())),
                                 preferred_element_type=F32)
            if j < KV_BLOCKS - 1:
                valid = t + (j - (KV_BLOCKS - 1)) >= 0
                off = pl.multiple_of(jnp.where(valid, j * Q_TILE, BAND_KEYS), Q_TILE)
            else:
                off = j * Q_TILE
            s.append(sj + bias_ref[h, :, pl.ds(off, Q_TILE)])
        m = jnp.max(jnp.maximum(jnp.maximum(s[0], s[1]), s[2]), axis=-1, keepdims=True)
        oe = None
        for j in range(KV_BLOCKS):
            pj = jnp.exp2(s[j] - m).astype(BF16)
            part = jnp.dot(pj, vext_ref[h, j * Q_TILE:(j + 1) * Q_TILE, :],
                           preferred_element_type=F32)
            oe = part if oe is None else oe + part
        y = oe[:, 0:head_dim] / oe[:, head_dim:]
        ss = ss + jnp.sum(y * y, axis=-1, keepdims=True)
        y_ref[:, cols] = y * (g_ref[:, cols] * _silu(z_ref[:, cols].astype(F32)))

    rinv = lax.rsqrt(ss * (1.0 / (n_heads * head_dim)) + EPS)
    for r0 in range(0, Q_TILE, ROW_CHUNK):
        rows = slice(r0, r0 + ROW_CHUNK)
        o_ref[rows, :] = (y_ref[rows, :] * rinv[rows, :]).astype(o_ref.dtype)


def _bias_rows(table):
    e = np.arange(BIAS_ROW)
    e = np.where(e >= BAND_KEYS, e - BIAS_ROW, e)
    idx = np.clip((KV_BLOCKS - 1) * Q_TILE - e, -MAX_REL, MAX_REL) + MAX_REL
    return table[:, idx].astype(F32) * LOG2_E


def _attention(proj, brow, g, *, batch, seq, width, n_heads):
    m = proj.shape[0]
    tiles = seq // Q_TILE
    head_dim = width // n_heads

    def spec(off, col):
        return pl.BlockSpec((Q_TILE, width),
                            lambda b, t: (b * tiles + jnp.maximum(t + off, 0), col))

    in_specs = [spec(0, 0),
                spec(-2, 1), spec(-1, 1), spec(0, 1),
                spec(-2, 2), spec(-1, 2), spec(0, 2),
                spec(0, 3),
                pl.BlockSpec(brow.shape, lambda b, t: (0, 0)),
                pl.BlockSpec((1, width), lambda b, t: (0, 0))]
    return pl.pallas_call(
        functools.partial(_attn_kernel, n_heads=n_heads, head_dim=head_dim),
        out_shape=jax.ShapeDtypeStruct((m, width), BF16),
        grid=(batch, tiles),
        in_specs=in_specs,
        out_specs=pl.BlockSpec((Q_TILE, width), lambda b, t: (b * tiles + t, 0)),
        scratch_shapes=[pltpu.VMEM((Q_TILE, width), F32),
                        pltpu.VMEM((n_heads, Q_TILE, BAND_KEYS + Q_TILE), F32),
                        pltpu.VMEM((n_heads, BAND_KEYS, 2 * head_dim), BF16)],
        compiler_params=_cparams(2),
        name="band_attention",
    )(*([proj] * 8), brow, g.reshape(1, width))


def _conv_kernel(a_ref, gl_ref, z_ref, wdw_ref, bdw_ref, lng_ref, lnb_ref, wpw_ref, bpw_ref, g_ref,
                 o_ref, u2_ref, c_ref, yb_ref, pw_ref, *, strip_rows):
    tt, width = a_ref.shape
    n_slabs = width // LANES
    t = pl.program_id(1)

    @pl.when(t == 0)
    def _():
        u2_ref[...] = jnp.zeros_like(u2_ref)

    @pl.when(t > 0)
    def _():
        u2_ref[:, 0:2 * HALO, :] = u2_ref[:, 2 * tt:2 * (tt + HALO), :]

    def glu_rows(rows):
        u = a_ref[rows, :].astype(F32) * _sigmoid(gl_ref[rows, :].astype(F32))
        for c in range(n_slabs):
            u2_ref[c, pl.ds(2 * (HALO + rows.start), ROW_CHUNK, stride=2), :] = (
                u[:, c * LANES:(c + 1) * LANES])
    _row_loop(tt, glu_rows)

    def conv_slab(c, carry):
        cols = pl.ds(pl.multiple_of(c * LANES, LANES), LANES)
        for r0 in range(0, tt, strip_rows):
            acc = jnp.zeros((strip_rows, LANES), F32)
            for k in range(CONV_K):
                first = r0 + HALO - (CONV_K - 1) + k
                acc += u2_ref[c, pl.ds(2 * first, strip_rows, stride=2), :] * wdw_ref[k:k + 1, cols]
            c_ref[r0:r0 + strip_rows, cols] = acc + bdw_ref[:, cols]
        return carry
    lax.fori_loop(0, n_slabs, conv_slab, 0)

    def norm_rows(rows):
        c = c_ref[rows, :]
        mu = jnp.mean(c, axis=-1, keepdims=True)
        cc = c - mu
        var = jnp.mean(cc * cc, axis=-1, keepdims=True)
        y = cc * lax.rsqrt(var + EPS) * lng_ref[...] + lnb_ref[...]
        yb_ref[rows, :] = _silu(y).astype(BF16)
    _row_loop(tt, norm_rows, unroll=4)

    pw_ref[...] = jnp.dot(yb_ref[...], wpw_ref[...], preferred_element_type=F32)

    def finish_rows(rows):
        pw = pw_ref[rows, :] + bpw_ref[...]
        ms = jnp.mean(pw * pw, axis=-1, keepdims=True)
        z = z_ref[rows, :].astype(F32)
        o_ref[rows, :] = (pw * lax.rsqrt(ms + EPS) * g_ref[...] * _silu(z)).astype(o_ref.dtype)
    _row_loop(tt, finish_rows, unroll=2)


def _conv_module(proj, w_dw, b_dw, ln_g, ln_b, w_pw_bf, b_pw, g, *, batch, seq, width, col0, tt):
    m = proj.shape[0]
    tiles = seq // tt
    w_dw_p = jnp.zeros((HALO, width), F32).at[:CONV_K].set(w_dw)

    def spec(col):
        return pl.BlockSpec((tt, width), lambda b, t: (b * tiles + t, col))

    def vec():
        return pl.BlockSpec((1, width), lambda b, t: (0, 0))

    return pl.pallas_call(
        functools.partial(_conv_kernel, strip_rows=128),
        out_shape=jax.ShapeDtypeStruct((m, width), BF16),
        grid=(batch, tiles),
        in_specs=[spec(col0), spec(col0 + 1), spec(col0 + 2),
                  pl.BlockSpec((HALO, width), lambda b, t: (0, 0)),
                  vec(), vec(), vec(),
                  pl.BlockSpec((width, width), lambda b, t: (0, 0)),
                  vec(), vec()],
        out_specs=pl.BlockSpec((tt, width), lambda b, t: (b * tiles + t, 0)),
        scratch_shapes=[pltpu.VMEM((width // LANES, 2 * (tt + HALO), LANES), F32),
                        pltpu.VMEM((tt, width), F32),
                        pltpu.VMEM((tt, width), BF16),
                        pltpu.VMEM((tt, width), F32)],
        compiler_params=_cparams(2),
        name="conv_module",
    )(proj, proj, proj, w_dw_p, b_dw.reshape(1, width), ln_g.reshape(1, width),
      ln_b.reshape(1, width), w_pw_bf, b_pw.reshape(1, width), g.reshape(1, width))


def _cast_pair_kernel(a_ref, b_ref, o_ref):
    @pl.when(pl.program_id(0) == 0)
    def _():
        o_ref[...] = a_ref[...].astype(o_ref.dtype)

    @pl.when(pl.program_id(0) == 1)
    def _():
        o_ref[...] = b_ref[...].astype(o_ref.dtype)


def _cast_pair(a, b, *, rows):
    r, c = a.shape
    nr = r // rows
    return pl.pallas_call(
        _cast_pair_kernel,
        out_shape=jax.ShapeDtypeStruct((2, r, c), BF16),
        grid=(2, nr),
        in_specs=[pl.BlockSpec((rows, c), lambda s, t: (jnp.where(s == 0, t, nr - 1), 0)),
                  pl.BlockSpec((rows, c), lambda s, t: (jnp.where(s == 1, t, 0), 0))],
        out_specs=pl.BlockSpec((None, rows, c), lambda s, t: (s, t, 0)),
        compiler_params=_cparams(2),
        name="cast_weight_pair",
    )(a, b)


def _out_kernel(x_ref, ya_ref, yc_ref, w_ref, pg_ref, bg_ref, p_ref, wp_ref, fg_ref,
                o_ref, h_ref, hn_ref, ssh_ref, sso_ref, *, nj, tn, n_blocks, apply_final):
    i = pl.program_id(0)
    j = pl.program_id(1)
    cur = i % 2
    prev = 1 - cur
    wa = ya_ref.shape[1]
    d = hn_ref.shape[1]
    to = o_ref.shape[1]

    @pl.when((i < n_blocks) & (j == 0))
    def _():
        ssh_ref[...] = jnp.zeros_like(ssh_ref)
        sso_ref[cur] = jnp.zeros(sso_ref.shape[1:], F32)

    @pl.when((i < n_blocks) & (j < nj))
    def _():
        cols = pl.ds(pl.multiple_of(j * tn, tn), tn)
        acc = jnp.dot(ya_ref[...], w_ref[0:wa, :], preferred_element_type=F32)
        acc += jnp.dot(yc_ref[...], w_ref[wa:, :], preferred_element_type=F32)
        h = x_ref[...] + acc
        h_ref[cur, :, cols] = h
        hn_ref[:, cols] = (h * pg_ref[:, cols]).astype(BF16)
        ssh_ref[...] += jnp.sum(h * h, axis=-1, keepdims=True)

    @pl.when((i < n_blocks) & (j >= nj))
    def _():
        cols = pl.ds(pl.multiple_of((j - nj) * tn, tn), tn)
        rinv = lax.rsqrt(ssh_ref[...] * (1.0 / d) + EPS)
        gate = _sigmoid(
            jnp.dot(hn_ref[...], w_ref[...], preferred_element_type=F32) * rinv + bg_ref[...])
        pe = jnp.dot(p_ref[...].astype(BF16), wp_ref[...], preferred_element_type=F32)
        h2 = h_ref[cur, :, cols] + gate * pe
        h_ref[cur, :, cols] = h2
        sso_ref[cur] += jnp.sum(h2 * h2, axis=-1, keepdims=True)

    @pl.when(i > 0)
    def _():
        ocols = pl.ds(pl.multiple_of(j * to, to), to)
        y = h_ref[prev, :, ocols]
        if apply_final:
            y = y * lax.rsqrt(sso_ref[prev] * (1.0 / d) + EPS) * fg_ref[:, ocols]
        o_ref[...] = y


def _out_block(x2, y_a, y_c, w_pair_bf, ple_g, b_gate, p2, w_ple_bf, final_g, *, tm, tn, apply_final):
    m, d = x2.shape
    wa = y_a.shape[1]
    wc = y_c.shape[1]
    pd = p2.shape[1]
    nj = d // tn
    nb = m // tm
    to = d // (2 * nj)

    def row(i):
        return jnp.minimum(i, nb - 1)

    def first(j):
        return jnp.minimum(j, nj - 1)

    def second(j):
        return jnp.maximum(j - nj, 0)

    def w_block(i, j):
        jj = jnp.where(i < nb, j, 2 * nj - 1)
        return (jj // nj, 0, jj % nj)

    return pl.pallas_call(
        functools.partial(_out_kernel, nj=nj, tn=tn, n_blocks=nb, apply_final=apply_final),
        out_shape=jax.ShapeDtypeStruct((m, d), F32),
        grid=(nb + 1, 2 * nj),
        in_specs=[
            pl.BlockSpec((tm, tn), lambda i, j: (row(i), first(j))),
            pl.BlockSpec((tm, wa), lambda i, j: (row(i), 0)),
            pl.BlockSpec((tm, wc), lambda i, j: (row(i), 0)),
            pl.BlockSpec((None, d, tn), w_block),
            pl.BlockSpec((1, d), lambda i, j: (0, 0)),
            pl.BlockSpec((1, tn), lambda i, j: (0, second(j))),
            pl.BlockSpec((tm, pd), lambda i, j: (row(i), 0)),
            pl.BlockSpec((pd, tn), lambda i, j: (0, second(j))),
            pl.BlockSpec((1, d), lambda i, j: (0, 0)),
        ],
        out_specs=pl.BlockSpec((tm, to),
                               lambda i, j: (jnp.maximum(i - 1, 0), jnp.where(i == 0, 0, j))),
        scratch_shapes=[pltpu.VMEM((2, tm, d), F32), pltpu.VMEM((tm, d), BF16),
                        pltpu.VMEM((tm, 1), F32), pltpu.VMEM((2, tm, 1), F32)],
        compiler_params=_cparams(2),
        name="out_block",
    )(x2, y_a, y_c, w_pair_bf, ple_g.reshape(1, d), b_gate.reshape(1, d), p2, w_ple_bf,
      final_g.reshape(1, d))


def kernel(x, p, norm_in_g, w_in, rel_table, w_dw, b_dw, conv_ln_g, conv_ln_b, w_pw, b_pw,
           attn_out_g, conv_out_g, w_out, ple_norm_g, w_ple_gate, b_ple_gate, w_ple, final_g):
    batch, seq, d = x.shape
    depth = w_in.shape[0]
    conv_width = w_pw.shape[1]
    att_width = d - conv_width
    m = batch * seq
    assert depth >= 1
    assert seq % Q_TILE == 0 and Q_TILE % CHUNK == 0
    assert (KV_BLOCKS - 1) * Q_TILE == LEFT_CHUNKS * CHUNK
    assert att_width == conv_width
    head_dim = att_width // N_HEADS
    col_scale = jnp.where(jnp.arange(w_in.shape[2]) < att_width,
                          head_dim ** -0.5 * LOG2_E, 1.0).astype(F32)

    h = x.reshape(m, d)
    for i in range(depth):
        last = i == depth - 1
        proj = _in_projection(h, norm_in_g[i], (w_in[i] * col_scale).astype(BF16),
                              tm=min(PROJ_TILE[0], m), tn=min(PROJ_TILE[1], d))
        y_a = _attention(proj, _bias_rows(rel_table[i]), attn_out_g[i],
                         batch=batch, seq=seq, width=att_width, n_heads=N_HEADS)
        y_c = _conv_module(proj, w_dw[i], b_dw[i], conv_ln_g[i], conv_ln_b[i],
                           w_pw[i].astype(BF16), b_pw[i], conv_out_g[i],
                           batch=batch, seq=seq, width=conv_width, col0=4, tt=min(CONV_ROWS, seq))
        w_pair = _cast_pair(w_out[i], w_ple_gate[i], rows=min(CAST_ROWS, d))
        h = _out_block(h, y_a, y_c, w_pair, ple_norm_g[i], b_ple_gate[i], p[i].reshape(m, -1),
                       w_ple[i].astype(BF16), final_g, tm=min(OUT_TILE[0], m),
                       tn=min(OUT_TILE[1], d), apply_final=last)
    return h.reshape(batch, seq, d)
```

```python
import functools

import numpy as np
import jax
import jax.numpy as jnp
from jax import lax
from jax.experimental import pallas as pl
from jax.experimental.pallas import tpu as pltpu

CHUNK = 64
LEFT_CHUNKS = 8
N_HEADS = 16
MAX_REL = 256
CONV_K = 31
EPS = 1e-6
NEG_INF = -1e30
LOG2_E = 1.4426950408889634

LANES = 128
Q_TILE = 256
KV_BLOCKS = 3
BAND_KEYS = KV_BLOCKS * Q_TILE
BIAS_ROW = 1024
HALO = 32
ROW_CHUNK = 32

V7X_VMEM_BYTES = 64 * 1024 * 1024
VMEM_LIMIT = V7X_VMEM_BYTES - 4 * 1024 * 1024

PROJ_TILE = (1024, 1024)
OUT_TILE = (512, 1024)
CONV_ROWS = 256
CAST_ROWS = 256

F32 = jnp.float32
BF16 = jnp.bfloat16


def _cparams(n_axes):
    return pltpu.CompilerParams(
        dimension_semantics=("arbitrary",) * n_axes, vmem_limit_bytes=VMEM_LIMIT)


def _row_loop(n_rows, body, unroll=1):
    def step(r, carry):
        body(pl.ds(pl.multiple_of(r * ROW_CHUNK, ROW_CHUNK), ROW_CHUNK))
        return carry
    lax.fori_loop(0, n_rows // ROW_CHUNK, step, 0, unroll=unroll)


def _sigmoid(z):
    return 0.5 * jnp.tanh(0.5 * z) + 0.5


def _silu(z):
    return z * _sigmoid(z)


def _proj_kernel(x_hbm, g_ref, w_ref, o_ref, x_buf, xn_ref, sem):
    i = pl.program_id(0)
    j = pl.program_id(1)
    tm = x_buf.shape[0]

    def x_copy(block):
        rows = pl.ds(pl.multiple_of(block * tm, tm), tm)
        return pltpu.make_async_copy(x_hbm.at[rows, :], x_buf, sem)

    @pl.when((i == 0) & (j == 0))
    def _():
        x_copy(0).start()

    @pl.when(j == 0)
    def _():
        x_copy(i).wait()

        def norm_rows(rows):
            x = x_buf[rows, :]
            ms = jnp.mean(x * x, axis=-1, keepdims=True)
            xn_ref[rows, :] = (x * lax.rsqrt(ms + EPS) * g_ref[...]).astype(BF16)
        _row_loop(tm, norm_rows, unroll=2)

    @pl.when((j == 1) & (i + 1 < pl.num_programs(0)))
    def _():
        x_copy(i + 1).start()

    o_ref[...] = jnp.dot(xn_ref[...], w_ref[...], preferred_element_type=F32).astype(o_ref.dtype)


def _in_projection(x2, g, w_bf, *, tm, tn, group):
    m, d = x2.shape
    n = w_bf.shape[1]
    assert n // tn >= 2
    gb = group // tn

    def out_block(i, j):
        return (i, jnp.where(j < gb, j + 2 * gb, jnp.where(j < 3 * gb, j - gb, j)))

    return pl.pallas_call(
        _proj_kernel,
        out_shape=jax.ShapeDtypeStruct((m, n), BF16),
        grid=(m // tm, n // tn),
        in_specs=[
            pl.BlockSpec(memory_space=pl.ANY),
            pl.BlockSpec((1, d), lambda i, j: (0, 0)),
            pl.BlockSpec((d, tn), lambda i, j: (0, j)),
        ],
        out_specs=pl.BlockSpec((tm, tn), out_block),
        scratch_shapes=[pltpu.VMEM((tm, d), F32), pltpu.VMEM((tm, d), BF16),
                        pltpu.SemaphoreType.DMA(())],
        compiler_params=_cparams(2),
        name="in_projection",
    )(x2, g.reshape(1, d), w_bf)


def _attn_kernel(kv0_ref, kv1_ref, kv2_ref, qz_ref, brow_ref, g_ref,
                 o_ref, y_ref, bias_ref, vext_ref, *, n_heads, head_dim):
    t = pl.program_id(1)
    width = n_heads * head_dim
    kv_refs = (kv0_ref, kv1_ref, kv2_ref)

    @pl.when((pl.program_id(0) == 0) & (t == 0))
    def _():
        qi = lax.broadcasted_iota(jnp.int32, (Q_TILE, BAND_KEYS), 0) // CHUNK
        kj = lax.broadcasted_iota(jnp.int32, (Q_TILE, BAND_KEYS), 1) // CHUNK
        in_band = (kj >= qi) & (kj <= qi + LEFT_CHUNKS)
        for h in range(n_heads):
            row = jnp.broadcast_to(brow_ref[h:h + 1, :], (Q_TILE, BIAS_ROW))
            skew = pltpu.roll(row, 0, 1, stride=1, stride_axis=0)
            bias_ref[h, :, 0:BAND_KEYS] = jnp.where(in_band, skew[:, :BAND_KEYS], NEG_INF)
            bias_ref[h, :, BAND_KEYS:] = jnp.full((Q_TILE, Q_TILE), NEG_INF, F32)
        vext_ref[:, :, head_dim:] = jnp.ones((n_heads, BAND_KEYS, head_dim), BF16)

    ss = jnp.zeros((Q_TILE, 1), F32)
    for h in range(n_heads):
        cols = slice(h * head_dim, (h + 1) * head_dim)
        cols2 = slice(width + h * head_dim, width + (h + 1) * head_dim)
        q = qz_ref[:, cols]
        s = []
        for j in range(KV_BLOCKS):
            vext_ref[h, j * Q_TILE:(j + 1) * Q_TILE, 0:head_dim] = kv_refs[j][:, cols2]
            sj = lax.dot_general(q, kv_refs[j][:, cols], (((1,), (1,)), ((), ())),
                                 preferred_element_type=F32)
            if j < KV_BLOCKS - 1:
                valid = t + (j - (KV_BLOCKS - 1)) >= 0
                off = pl.multiple_of(jnp.where(valid, j * Q_TILE, BAND_KEYS), Q_TILE)
            else:
                off = j * Q_TILE
            s.append(sj + bias_ref[h, :, pl.ds(off, Q_TILE)])
        m = jnp.max(jnp.maximum(jnp.maximum(s[0], s[1]), s[2]), axis=-1, keepdims=True)
        oe = None
        for j in range(KV_BLOCKS):
            pj = jnp.exp2(s[j] - m).astype(BF16)
            part = jnp.dot(pj, vext_ref[h, j * Q_TILE:(j + 1) * Q_TILE, :],
                           preferred_element_type=F32)
            oe = part if oe is None else oe + part
        y = oe[:, 0:head_dim] / oe[:, head_dim:]
        ss = ss + jnp.sum(y * y, axis=-1, keepdims=True)
        y_ref[:, cols] = y * (g_ref[:, cols] * _silu(qz_ref[:, cols2].astype(F32)))

    rinv = lax.rsqrt(ss * (1.0 / width) + EPS)
    for r0 in range(0, Q_TILE, ROW_CHUNK):
        rows = slice(r0, r0 + ROW_CHUNK)
        o_ref[rows, :] = (y_ref[rows, :] * rinv[rows, :]).astype(o_ref.dtype)


def _bias_rows(table):
    e = np.arange(BIAS_ROW)
    e = np.where(e >= BAND_KEYS, e - BIAS_ROW, e)
    idx = np.clip((KV_BLOCKS - 1) * Q_TILE - e, -MAX_REL, MAX_REL) + MAX_REL
    return table[:, idx].astype(F32) * LOG2_E


def _attention(proj, brow, g, *, batch, seq, width, n_heads):
    m = proj.shape[0]
    tiles = seq // Q_TILE
    head_dim = width // n_heads

    def spec(off, col):
        return pl.BlockSpec((Q_TILE, 2 * width),
                            lambda b, t: (b * tiles + jnp.maximum(t + off, 0), col))

    in_specs = [spec(-2, 0), spec(-1, 0), spec(0, 0),
                spec(0, 1),
                pl.BlockSpec(brow.shape, lambda b, t: (0, 0)),
                pl.BlockSpec((1, width), lambda b, t: (0, 0))]
    return pl.pallas_call(
        functools.partial(_attn_kernel, n_heads=n_heads, head_dim=head_dim),
        out_shape=jax.ShapeDtypeStruct((m, width), BF16),
        grid=(batch, tiles),
        in_specs=in_specs,
        out_specs=pl.BlockSpec((Q_TILE, width), lambda b, t: (b * tiles + t, 0)),
        scratch_shapes=[pltpu.VMEM((Q_TILE, width), F32),
                        pltpu.VMEM((n_heads, Q_TILE, BAND_KEYS + Q_TILE), F32),
                        pltpu.VMEM((n_heads, BAND_KEYS, 2 * head_dim), BF16)],
        compiler_params=_cparams(2),
        name="band_attention",
    )(*([proj] * 4), brow, g.reshape(1, width))


def _conv_kernel(a_ref, gl_ref, z_ref, wdw_ref, bdw_ref, lng_ref, lnb_ref, wpw_ref, bpw_ref, g_ref,
                 o_ref, u2_ref, c_ref, yb_ref, pw_ref, *, strip_rows):
    tt, width = a_ref.shape
    n_slabs = width // LANES
    t = pl.program_id(1)

    @pl.when(t == 0)
    def _():
        u2_ref[...] = jnp.zeros_like(u2_ref)

    @pl.when(t > 0)
    def _():
        u2_ref[:, 0:2 * HALO, :] = u2_ref[:, 2 * tt:2 * (tt + HALO), :]

    def glu_rows(rows):
        u = a_ref[rows, :].astype(F32) * _sigmoid(gl_ref[rows, :].astype(F32))
        for c in range(n_slabs):
            u2_ref[c, pl.ds(2 * (HALO + rows.start), ROW_CHUNK, stride=2), :] = (
                u[:, c * LANES:(c + 1) * LANES])
    _row_loop(tt, glu_rows)

    def conv_slab(c, carry):
        cols = pl.ds(pl.multiple_of(c * LANES, LANES), LANES)
        for r0 in range(0, tt, strip_rows):
            acc = jnp.zeros((strip_rows, LANES), F32)
            for k in range(CONV_K):
                first = r0 + HALO - (CONV_K - 1) + k
                acc += u2_ref[c, pl.ds(2 * first, strip_rows, stride=2), :] * wdw_ref[k:k + 1, cols]
            c_ref[r0:r0 + strip_rows, cols] = acc + bdw_ref[:, cols]
        return carry
    lax.fori_loop(0, n_slabs, conv_slab, 0)

    def norm_rows(rows):
        c = c_ref[rows, :]
        mu = jnp.mean(c, axis=-1, keepdims=True)
        cc = c - mu
        var = jnp.mean(cc * cc, axis=-1, keepdims=True)
        y = cc * lax.rsqrt(var + EPS) * lng_ref[...] + lnb_ref[...]
        yb_ref[rows, :] = _silu(y).astype(BF16)
    _row_loop(tt, norm_rows, unroll=4)

    pw_ref[...] = jnp.dot(yb_ref[...], wpw_ref[...], preferred_element_type=F32)

    def finish_rows(rows):
        pw = pw_ref[rows, :] + bpw_ref[...]
        ms = jnp.mean(pw * pw, axis=-1, keepdims=True)
        z = z_ref[rows, :].astype(F32)
        o_ref[rows, :] = (pw * lax.rsqrt(ms + EPS) * g_ref[...] * _silu(z)).astype(o_ref.dtype)
    _row_loop(tt, finish_rows, unroll=2)


def _conv_module(proj, w_dw, b_dw, ln_g, ln_b, w_pw_bf, b_pw, g, *, batch, seq, width, col0, tt):
    m = proj.shape[0]
    tiles = seq // tt
    w_dw_p = jnp.zeros((HALO, width), F32).at[:CONV_K].set(w_dw)

    def spec(col):
        return pl.BlockSpec((tt, width), lambda b, t: (b * tiles + t, col))

    def vec():
        return pl.BlockSpec((1, width), lambda b, t: (0, 0))

    return pl.pallas_call(
        functools.partial(_conv_kernel, strip_rows=128),
        out_shape=jax.ShapeDtypeStruct((m, width), BF16),
        grid=(batch, tiles),
        in_specs=[spec(col0), spec(col0 + 1), spec(col0 + 2),
                  pl.BlockSpec((HALO, width), lambda b, t: (0, 0)),
                  vec(), vec(), vec(),
                  pl.BlockSpec((width, width), lambda b, t: (0, 0)),
                  vec(), vec()],
        out_specs=pl.BlockSpec((tt, width), lambda b, t: (b * tiles + t, 0)),
        scratch_shapes=[pltpu.VMEM((width // LANES, 2 * (tt + HALO), LANES), F32),
                        pltpu.VMEM((tt, width), F32),
                        pltpu.VMEM((tt, width), BF16),
                        pltpu.VMEM((tt, width), F32)],
        compiler_params=_cparams(2),
        name="conv_module",
    )(proj, proj, proj, w_dw_p, b_dw.reshape(1, width), ln_g.reshape(1, width),
      ln_b.reshape(1, width), w_pw_bf, b_pw.reshape(1, width), g.reshape(1, width))


def _cast_pair_kernel(a_ref, b_ref, o_ref):
    @pl.when(pl.program_id(0) == 0)
    def _():
        o_ref[...] = a_ref[...].astype(o_ref.dtype)

    @pl.when(pl.program_id(0) == 1)
    def _():
        o_ref[...] = b_ref[...].astype(o_ref.dtype)


def _cast_pair(a, b, *, rows):
    r, c = a.shape
    nr = r // rows
    return pl.pallas_call(
        _cast_pair_kernel,
        out_shape=jax.ShapeDtypeStruct((2, r, c), BF16),
        grid=(2, nr),
        in_specs=[pl.BlockSpec((rows, c), lambda s, t: (jnp.where(s == 0, t, nr - 1), 0)),
                  pl.BlockSpec((rows, c), lambda s, t: (jnp.where(s == 1, t, 0), 0))],
        out_specs=pl.BlockSpec((None, rows, c), lambda s, t: (s, t, 0)),
        compiler_params=_cparams(2),
        name="cast_weight_pair",
    )(a, b)


def _out_kernel(x_ref, ya_ref, yc_ref, w_ref, pg_ref, bg_ref, p_ref, wp_ref, fg_ref,
                o_ref, h_ref, hn_ref, ssh_ref, sso_ref, *, nj, tn, n_blocks, apply_final):
    i = pl.program_id(0)
    j = pl.program_id(1)
    cur = i % 2
    prev = 1 - cur
    wa = ya_ref.shape[1]
    d = hn_ref.shape[1]
    to = o_ref.shape[1]

    @pl.when((i < n_blocks) & (j == 0))
    def _():
        ssh_ref[...] = jnp.zeros_like(ssh_ref)
        sso_ref[cur] = jnp.zeros(sso_ref.shape[1:], F32)

    @pl.when((i < n_blocks) & (j < nj))
    def _():
        cols = pl.ds(pl.multiple_of(j * tn, tn), tn)
        acc = jnp.dot(ya_ref[...], w_ref[0:wa, :], preferred_element_type=F32)
        acc += jnp.dot(yc_ref[...], w_ref[wa:, :], preferred_element_type=F32)
        h = x_ref[...] + acc
        h_ref[cur, :, cols] = h
        hn_ref[:, cols] = (h * pg_ref[:, cols]).astype(BF16)
        ssh_ref[...] += jnp.sum(h * h, axis=-1, keepdims=True)

    @pl.when((i < n_blocks) & (j >= nj))
    def _():
        cols = pl.ds(pl.multiple_of((j - nj) * tn, tn), tn)
        rinv = lax.rsqrt(ssh_ref[...] * (1.0 / d) + EPS)
        gate = _sigmoid(
            jnp.dot(hn_ref[...], w_ref[...], preferred_element_type=F32) * rinv + bg_ref[...])
        pe = jnp.dot(p_ref[...].astype(BF16), wp_ref[...], preferred_element_type=F32)
        h2 = h_ref[cur, :, cols] + gate * pe
        h_ref[cur, :, cols] = h2
        sso_ref[cur] += jnp.sum(h2 * h2, axis=-1, keepdims=True)

    @pl.when(i > 0)
    def _():
        ocols = pl.ds(pl.multiple_of(j * to, to), to)
        y = h_ref[prev, :, ocols]
        if apply_final:
            y = y * lax.rsqrt(sso_ref[prev] * (1.0 / d) + EPS) * fg_ref[:, ocols]
        o_ref[...] = y


def _out_block(x2, y_a, y_c, w_pair_bf, ple_g, b_gate, p2, w_ple_bf, final_g, *, tm, tn, apply_final):
    m, d = x2.shape
    wa = y_a.shape[1]
    wc = y_c.shape[1]
    pd = p2.shape[1]
    nj = d // tn
    nb = m // tm
    to = d // (2 * nj)

    def row(i):
        return jnp.minimum(i, nb - 1)

    def first(j):
        return jnp.minimum(j, nj - 1)

    def second(j):
        return jnp.maximum(j - nj, 0)

    def w_block(i, j):
        jj = jnp.where(i < nb, j, 2 * nj - 1)
        return (jj // nj, 0, jj % nj)

    return pl.pallas_call(
        functools.partial(_out_kernel, nj=nj, tn=tn, n_blocks=nb, apply_final=apply_final),
        out_shape=jax.ShapeDtypeStruct((m, d), F32),
        grid=(nb + 1, 2 * nj),
        in_specs=[
            pl.BlockSpec((tm, tn), lambda i, j: (row(i), first(j))),
            pl.BlockSpec((tm, wa), lambda i, j: (row(i), 0)),
            pl.BlockSpec((tm, wc), lambda i, j: (row(i), 0)),
            pl.BlockSpec((None, d, tn), w_block),
            pl.BlockSpec((1, d), lambda i, j: (0, 0)),
            pl.BlockSpec((1, tn), lambda i, j: (0, second(j))),
            pl.BlockSpec((tm, pd), lambda i, j: (row(i), 0)),
            pl.BlockSpec((pd, tn), lambda i, j: (0, second(j))),
            pl.BlockSpec((1, d), lambda i, j: (0, 0)),
        ],
        out_specs=pl.BlockSpec((tm, to),
                               lambda i, j: (jnp.maximum(i - 1, 0), jnp.where(i == 0, 0, j))),
        scratch_shapes=[pltpu.VMEM((2, tm, d), F32), pltpu.VMEM((tm, d), BF16),
                        pltpu.VMEM((tm, 1), F32), pltpu.VMEM((2, tm, 1), F32)],
        compiler_params=_cparams(2),
        name="out_block",
    )(x2, y_a, y_c, w_pair_bf, ple_g.reshape(1, d), b_gate.reshape(1, d), p2, w_ple_bf,
      final_g.reshape(1, d))


def kernel(x, p, norm_in_g, w_in, rel_table, w_dw, b_dw, conv_ln_g, conv_ln_b, w_pw, b_pw,
           attn_out_g, conv_out_g, w_out, ple_norm_g, w_ple_gate, b_ple_gate, w_ple, final_g):
    batch, seq, d = x.shape
    depth = w_in.shape[0]
    conv_width = w_pw.shape[1]
    att_width = d - conv_width
    m = batch * seq
    assert depth >= 1
    assert seq % Q_TILE == 0 and Q_TILE % CHUNK == 0
    assert (KV_BLOCKS - 1) * Q_TILE == LEFT_CHUNKS * CHUNK
    assert att_width == conv_width
    head_dim = att_width // N_HEADS
    col_scale = jnp.where(jnp.arange(w_in.shape[2]) < att_width,
                          head_dim ** -0.5 * LOG2_E, 1.0).astype(F32)

    h = x.reshape(m, d)
    for i in range(depth):
        last = i == depth - 1
        proj = _in_projection(h, norm_in_g[i], (w_in[i] * col_scale).astype(BF16),
                              tm=min(PROJ_TILE[0], m), tn=min(PROJ_TILE[1], att_width),
                              group=att_width)
        y_a = _attention(proj, _bias_rows(rel_table[i]), attn_out_g[i],
                         batch=batch, seq=seq, width=att_width, n_heads=N_HEADS)
        y_c = _conv_module(proj, w_dw[i], b_dw[i], conv_ln_g[i], conv_ln_b[i],
                           w_pw[i].astype(BF16), b_pw[i], conv_out_g[i],
                           batch=batch, seq=seq, width=conv_width, col0=4, tt=min(CONV_ROWS, seq))
        w_pair = _cast_pair(w_out[i], w_ple_gate[i], rows=min(CAST_ROWS, d))
        h = _out_block(h, y_a, y_c, w_pair, ple_norm_g[i], b_ple_gate[i], p[i].reshape(m, -1),
                       w_ple[i].astype(BF16), final_g, tm=min(OUT_TILE[0], m),
                       tn=min(OUT_TILE[1], d), apply_final=last)
    return h.reshape(batch, seq, d)
```

```python
import functools

import numpy as np
import jax
import jax.numpy as jnp
from jax import lax
from jax.experimental import pallas as pl
from jax.experimental.pallas import tpu as pltpu

CHUNK = 64
LEFT_CHUNKS = 8
N_HEADS = 16
MAX_REL = 256
CONV_K = 31
EPS = 1e-6
NEG_INF = -1e30
LOG2_E = 1.4426950408889634

LANES = 128
Q_TILE = 256
KV_BLOCKS = 3
BAND_KEYS = KV_BLOCKS * Q_TILE
BIAS_ROW = 1024
HALO = 32
ROW_CHUNK = 32

V7X_VMEM_BYTES = 64 * 1024 * 1024
VMEM_LIMIT = V7X_VMEM_BYTES - 4 * 1024 * 1024

PROJ_TILE = (1024, 1024)
OUT_TILE = (512, 1024)
CONV_ROWS = 256
CAST_ROWS = 256

F32 = jnp.float32
BF16 = jnp.bfloat16


def _cparams(n_axes):
    return pltpu.CompilerParams(
        dimension_semantics=("arbitrary",) * n_axes, vmem_limit_bytes=VMEM_LIMIT)


def _row_loop(n_rows, body, unroll=1):
    def step(r, carry):
        body(pl.ds(pl.multiple_of(r * ROW_CHUNK, ROW_CHUNK), ROW_CHUNK))
        return carry
    lax.fori_loop(0, n_rows // ROW_CHUNK, step, 0, unroll=unroll)


def _sigmoid(z):
    return 0.5 * jnp.tanh(0.5 * z) + 0.5


def _silu(z):
    return z * _sigmoid(z)


def _proj_kernel(x_hbm, g_ref, w_ref, o_ref, x_buf, xn_ref, rinv_ref, sem):
    i = pl.program_id(0)
    j = pl.program_id(1)
    tm = x_buf.shape[0]

    def x_copy(block):
        rows = pl.ds(pl.multiple_of(block * tm, tm), tm)
        return pltpu.make_async_copy(x_hbm.at[rows, :], x_buf, sem)

    @pl.when((i == 0) & (j == 0))
    def _():
        x_copy(0).start()

    @pl.when(j == 0)
    def _():
        x_copy(i).wait()

        def norm_rows(rows):
            x = x_buf[rows, :]
            rinv_ref[rows, :] = lax.rsqrt(jnp.mean(x * x, axis=-1, keepdims=True) + EPS)
            xn_ref[rows, :] = (x * g_ref[...]).astype(BF16)
        _row_loop(tm, norm_rows, unroll=2)

    @pl.when((j == 1) & (i + 1 < pl.num_programs(0)))
    def _():
        x_copy(i + 1).start()

    acc = jnp.dot(xn_ref[...], w_ref[...], preferred_element_type=F32)
    o_ref[...] = (acc * rinv_ref[...]).astype(o_ref.dtype)


def _in_projection(x2, g, w_bf, *, tm, tn, group):
    m, d = x2.shape
    n = w_bf.shape[1]
    assert n // tn >= 2
    gb = group // tn

    def out_block(i, j):
        return (i, jnp.where(j < gb, j + 2 * gb, jnp.where(j < 3 * gb, j - gb, j)))

    return pl.pallas_call(
        _proj_kernel,
        out_shape=jax.ShapeDtypeStruct((m, n), BF16),
        grid=(m // tm, n // tn),
        in_specs=[
            pl.BlockSpec(memory_space=pl.ANY),
            pl.BlockSpec((1, d), lambda i, j: (0, 0)),
            pl.BlockSpec((d, tn), lambda i, j: (0, j)),
        ],
        out_specs=pl.BlockSpec((tm, tn), out_block),
        scratch_shapes=[pltpu.VMEM((tm, d), F32), pltpu.VMEM((tm, d), BF16),
                        pltpu.VMEM((tm, 1), F32), pltpu.SemaphoreType.DMA(())],
        compiler_params=_cparams(2),
        name="in_projection",
    )(x2, g.reshape(1, d), w_bf)


def _attn_kernel(kv0_ref, kv1_ref, kv2_ref, qz_ref, brow_ref, g_ref,
                 o_ref, y_ref, bias_ref, vext_ref, *, n_heads, head_dim):
    t = pl.program_id(1)
    width = n_heads * head_dim
    kv_refs = (kv0_ref, kv1_ref, kv2_ref)

    @pl.when((pl.program_id(0) == 0) & (t == 0))
    def _():
        qi = lax.broadcasted_iota(jnp.int32, (Q_TILE, BAND_KEYS), 0) // CHUNK
        kj = lax.broadcasted_iota(jnp.int32, (Q_TILE, BAND_KEYS), 1) // CHUNK
        in_band = (kj >= qi) & (kj <= qi + LEFT_CHUNKS)
        for h in range(n_heads):
            row = jnp.broadcast_to(brow_ref[h:h + 1, :], (Q_TILE, BIAS_ROW))
            skew = pltpu.roll(row, 0, 1, stride=1, stride_axis=0)
            bias_ref[h, :, 0:BAND_KEYS] = jnp.where(in_band, skew[:, :BAND_KEYS], NEG_INF)
            bias_ref[h, :, BAND_KEYS:] = jnp.full((Q_TILE, Q_TILE), NEG_INF, F32)
        vext_ref[:, :, head_dim:] = jnp.ones((n_heads, BAND_KEYS, head_dim), BF16)

    ss = jnp.zeros((Q_TILE, 1), F32)
    for h in range(n_heads):
        cols = slice(h * head_dim, (h + 1) * head_dim)
        cols2 = slice(width + h * head_dim, width + (h + 1) * head_dim)
        q = qz_ref[:, cols]
        s = []
        for j in range(KV_BLOCKS):
            vext_ref[h, j * Q_TILE:(j + 1) * Q_TILE, 0:head_dim] = kv_refs[j][:, cols2]
            sj = lax.dot_general(q, kv_refs[j][:, cols], (((1,), (1,)), ((), ())),
                                 preferred_element_type=F32)
            if j < KV_BLOCKS - 1:
                valid = t + (j - (KV_BLOCKS - 1)) >= 0
                off = pl.multiple_of(jnp.where(valid, j * Q_TILE, BAND_KEYS), Q_TILE)
            else:
                off = j * Q_TILE
            s.append(sj + bias_ref[h, :, pl.ds(off, Q_TILE)])
        m = jnp.max(jnp.maximum(jnp.maximum(s[0], s[1]), s[2]), axis=-1, keepdims=True)
        oe = None
        for j in range(KV_BLOCKS):
            pj = jnp.exp2(s[j] - m).astype(BF16)
            part = jnp.dot(pj, vext_ref[h, j * Q_TILE:(j + 1) * Q_TILE, :],
                           preferred_element_type=F32)
            oe = part if oe is None else oe + part
        y = oe[:, 0:head_dim] / oe[:, head_dim:]
        ss = ss + jnp.sum(y * y, axis=-1, keepdims=True)
        y_ref[:, cols] = y * (g_ref[:, cols] * _silu(qz_ref[:, cols2].astype(F32)))

    rinv = lax.rsqrt(ss * (1.0 / width) + EPS)
    for r0 in range(0, Q_TILE, ROW_CHUNK):
        rows = slice(r0, r0 + ROW_CHUNK)
        o_ref[rows, :] = (y_ref[rows, :] * rinv[rows, :]).astype(o_ref.dtype)


def _bias_rows(table):
    e = np.arange(BIAS_ROW)
    e = np.where(e >= BAND_KEYS, e - BIAS_ROW, e)
    idx = np.clip((KV_BLOCKS - 1) * Q_TILE - e, -MAX_REL, MAX_REL) + MAX_REL
    return table[:, idx].astype(F32) * LOG2_E


def _attention(proj, brow, g, *, batch, seq, width, n_heads):
    m = proj.shape[0]
    tiles = seq // Q_TILE
    head_dim = width // n_heads

    def spec(off, col):
        return pl.BlockSpec((Q_TILE, 2 * width),
                            lambda b, t: (b * tiles + jnp.maximum(t + off, 0), col))

    in_specs = [spec(-2, 0), spec(-1, 0), spec(0, 0),
                spec(0, 1),
                pl.BlockSpec(brow.shape, lambda b, t: (0, 0)),
                pl.BlockSpec((1, width), lambda b, t: (0, 0))]
    return pl.pallas_call(
        functools.partial(_attn_kernel, n_heads=n_heads, head_dim=head_dim),
        out_shape=jax.ShapeDtypeStruct((m, width), BF16),
        grid=(batch, tiles),
        in_specs=in_specs,
        out_specs=pl.BlockSpec((Q_TILE, width), lambda b, t: (b * tiles + t, 0)),
        scratch_shapes=[pltpu.VMEM((Q_TILE, width), F32),
                        pltpu.VMEM((n_heads, Q_TILE, BAND_KEYS + Q_TILE), F32),
                        pltpu.VMEM((n_heads, BAND_KEYS, 2 * head_dim), BF16)],
        compiler_params=_cparams(2),
        name="band_attention",
    )(*([proj] * 4), brow, g.reshape(1, width))


def _conv_kernel(a_ref, gl_ref, z_ref, wdw_ref, bdw_ref, lng_ref, lnb_ref, wpw_ref, bpw_ref, g_ref,
                 o_ref, u2_ref, c_ref, yb_ref, pw_ref, *, strip_rows):
    tt, width = a_ref.shape
    n_slabs = width // LANES
    t = pl.program_id(1)

    @pl.when(t == 0)
    def _():
        u2_ref[...] = jnp.zeros_like(u2_ref)

    @pl.when(t > 0)
    def _():
        u2_ref[:, 0:2 * HALO, :] = u2_ref[:, 2 * tt:2 * (tt + HALO), :]

    def glu_rows(rows):
        u = a_ref[rows, :].astype(F32) * _sigmoid(gl_ref[rows, :].astype(F32))
        for c in range(n_slabs):
            u2_ref[c, pl.ds(2 * (HALO + rows.start), ROW_CHUNK, stride=2), :] = (
                u[:, c * LANES:(c + 1) * LANES])
    _row_loop(tt, glu_rows)

    def conv_slab(c, carry):
        cols = pl.ds(pl.multiple_of(c * LANES, LANES), LANES)
        for r0 in range(0, tt, strip_rows):
            acc = jnp.broadcast_to(bdw_ref[:, cols], (strip_rows, LANES))
            for k in range(CONV_K):
                first = r0 + HALO - (CONV_K - 1) + k
                acc += u2_ref[c, pl.ds(2 * first, strip_rows, stride=2), :] * wdw_ref[k:k + 1, cols]
            c_ref[r0:r0 + strip_rows, cols] = acc
        return carry
    lax.fori_loop(0, n_slabs, conv_slab, 0)

    def norm_rows(rows):
        c = c_ref[rows, :]
        mu = jnp.mean(c, axis=-1, keepdims=True)
        cc = c - mu
        var = jnp.mean(cc * cc, axis=-1, keepdims=True)
        y = cc * lax.rsqrt(var + EPS) * lng_ref[...] + lnb_ref[...]
        yb_ref[rows, :] = _silu(y).astype(BF16)
    _row_loop(tt, norm_rows, unroll=4)

    pw_ref[...] = jnp.dot(yb_ref[...], wpw_ref[...], preferred_element_type=F32)

    def finish_rows(rows):
        pw = pw_ref[rows, :] + bpw_ref[...]
        ms = jnp.mean(pw * pw, axis=-1, keepdims=True)
        z = z_ref[rows, :].astype(F32)
        o_ref[rows, :] = (pw * lax.rsqrt(ms + EPS) * g_ref[...] * _silu(z)).astype(o_ref.dtype)
    _row_loop(tt, finish_rows, unroll=2)


def _conv_module(proj, w_dw, b_dw, ln_g, ln_b, w_pw_bf, b_pw, g, *, batch, seq, width, col0, tt):
    m = proj.shape[0]
    tiles = seq // tt
    w_dw_p = jnp.zeros((HALO, width), F32).at[:CONV_K].set(w_dw)

    def spec(col):
        return pl.BlockSpec((tt, width), lambda b, t: (b * tiles + t, col))

    def vec():
        return pl.BlockSpec((1, width), lambda b, t: (0, 0))

    return pl.pallas_call(
        functools.partial(_conv_kernel, strip_rows=128),
        out_shape=jax.ShapeDtypeStruct((m, width), BF16),
        grid=(batch, tiles),
        in_specs=[spec(col0), spec(col0 + 1), spec(col0 + 2),
                  pl.BlockSpec((HALO, width), lambda b, t: (0, 0)),
                  vec(), vec(), vec(),
                  pl.BlockSpec((width, width), lambda b, t: (0, 0)),
                  vec(), vec()],
        out_specs=pl.BlockSpec((tt, width), lambda b, t: (b * tiles + t, 0)),
        scratch_shapes=[pltpu.VMEM((width // LANES, 2 * (tt + HALO), LANES), F32),
                        pltpu.VMEM((tt, width), F32),
                        pltpu.VMEM((tt, width), BF16),
                        pltpu.VMEM((tt, width), F32)],
        compiler_params=_cparams(2),
        name="conv_module",
    )(proj, proj, proj, w_dw_p, b_dw.reshape(1, width), ln_g.reshape(1, width),
      ln_b.reshape(1, width), w_pw_bf, b_pw.reshape(1, width), g.reshape(1, width))


def _cast_pair_kernel(a_ref, b_ref, o_ref):
    @pl.when(pl.program_id(0) == 0)
    def _():
        o_ref[...] = a_ref[...].astype(o_ref.dtype)

    @pl.when(pl.program_id(0) == 1)
    def _():
        o_ref[...] = b_ref[...].astype(o_ref.dtype)


def _cast_pair(a, b, *, rows):
    r, c = a.shape
    nr = r // rows
    return pl.pallas_call(
        _cast_pair_kernel,
        out_shape=jax.ShapeDtypeStruct((2, r, c), BF16),
        grid=(2, nr),
        in_specs=[pl.BlockSpec((rows, c), lambda s, t: (jnp.where(s == 0, t, nr - 1), 0)),
                  pl.BlockSpec((rows, c), lambda s, t: (jnp.where(s == 1, t, 0), 0))],
        out_specs=pl.BlockSpec((None, rows, c), lambda s, t: (s, t, 0)),
        compiler_params=_cparams(2),
        name="cast_weight_pair",
    )(a, b)


def _out_kernel(x_ref, ya_ref, yc_ref, w_ref, pg_ref, bg_ref, p_ref, wp_ref, fg_ref,
                o_ref, h_ref, hn_ref, ssh_ref, sso_ref, *, nj, tn, n_blocks, apply_final):
    i = pl.program_id(0)
    j = pl.program_id(1)
    cur = i % 2
    prev = 1 - cur
    wa = ya_ref.shape[1]
    d = hn_ref.shape[1]
    to = o_ref.shape[1]

    @pl.when((i < n_blocks) & (j == 0))
    def _():
        ssh_ref[...] = jnp.zeros_like(ssh_ref)
        sso_ref[cur] = jnp.zeros(sso_ref.shape[1:], F32)

    @pl.when((i < n_blocks) & (j < nj))
    def _():
        cols = pl.ds(pl.multiple_of(j * tn, tn), tn)
        acc = jnp.dot(ya_ref[...], w_ref[0:wa, :], preferred_element_type=F32)
        acc += jnp.dot(yc_ref[...], w_ref[wa:, :], preferred_element_type=F32)
        h = x_ref[...] + acc
        h_ref[cur, :, cols] = h
        hn_ref[:, cols] = (h * pg_ref[:, cols]).astype(BF16)
        ssh_ref[...] += jnp.sum(h * h, axis=-1, keepdims=True)

    @pl.when((i < n_blocks) & (j >= nj))
    def _():
        cols = pl.ds(pl.multiple_of((j - nj) * tn, tn), tn)
        rinv = lax.rsqrt(ssh_ref[...] * (1.0 / d) + EPS)
        gate = _sigmoid(
            jnp.dot(hn_ref[...], w_ref[...], preferred_element_type=F32) * rinv + bg_ref[...])
        pe = jnp.dot(p_ref[...].astype(BF16), wp_ref[...], preferred_element_type=F32)
        h2 = h_ref[cur, :, cols] + gate * pe
        h_ref[cur, :, cols] = h2
        sso_ref[cur] += jnp.sum(h2 * h2, axis=-1, keepdims=True)

    @pl.when(i > 0)
    def _():
        ocols = pl.ds(pl.multiple_of(j * to, to), to)
        y = h_ref[prev, :, ocols]
        if apply_final:
            y = y * lax.rsqrt(sso_ref[prev] * (1.0 / d) + EPS) * fg_ref[:, ocols]
        o_ref[...] = y


def _out_block(x2, y_a, y_c, w_pair_bf, ple_g, b_gate, p2, w_ple_bf, final_g, *, tm, tn, apply_final):
    m, d = x2.shape
    wa = y_a.shape[1]
    wc = y_c.shape[1]
    pd = p2.shape[1]
    nj = d // tn
    nb = m // tm
    to = d // (2 * nj)

    def row(i):
        return jnp.minimum(i, nb - 1)

    def first(j):
        return jnp.minimum(j, nj - 1)

    def second(j):
        return jnp.maximum(j - nj, 0)

    def w_block(i, j):
        jj = jnp.where(i < nb, j, 2 * nj - 1)
        return (jj // nj, 0, jj % nj)

    return pl.pallas_call(
        functools.partial(_out_kernel, nj=nj, tn=tn, n_blocks=nb, apply_final=apply_final),
        out_shape=jax.ShapeDtypeStruct((m, d), F32),
        grid=(nb + 1, 2 * nj),
        in_specs=[
            pl.BlockSpec((tm, tn), lambda i, j: (row(i), first(j))),
            pl.BlockSpec((tm, wa), lambda i, j: (row(i), 0)),
            pl.BlockSpec((tm, wc), lambda i, j: (row(i), 0)),
            pl.BlockSpec((None, d, tn), w_block),
            pl.BlockSpec((1, d), lambda i, j: (0, 0)),
            pl.BlockSpec((1, tn), lambda i, j: (0, second(j))),
            pl.BlockSpec((tm, pd), lambda i, j: (row(i), 0)),
            pl.BlockSpec((pd, tn), lambda i, j: (0, second(j))),
            pl.BlockSpec((1, d), lambda i, j: (0, 0)),
        ],
        out_specs=pl.BlockSpec((tm, to),
                               lambda i, j: (jnp.maximum(i - 1, 0), jnp.where(i == 0, 0, j))),
        scratch_shapes=[pltpu.VMEM((2, tm, d), F32), pltpu.VMEM((tm, d), BF16),
                        pltpu.VMEM((tm, 1), F32), pltpu.VMEM((2, tm, 1), F32)],
        compiler_params=_cparams(2),
        name="out_block",
    )(x2, y_a, y_c, w_pair_bf, ple_g.reshape(1, d), b_gate.reshape(1, d), p2, w_ple_bf,
      final_g.reshape(1, d))


def kernel(x, p, norm_in_g, w_in, rel_table, w_dw, b_dw, conv_ln_g, conv_ln_b, w_pw, b_pw,
           attn_out_g, conv_out_g, w_out, ple_norm_g, w_ple_gate, b_ple_gate, w_ple, final_g):
    batch, seq, d = x.shape
    depth = w_in.shape[0]
    conv_width = w_pw.shape[1]
    att_width = d - conv_width
    m = batch * seq
    assert depth >= 1
    assert seq % Q_TILE == 0 and Q_TILE % CHUNK == 0
    assert (KV_BLOCKS - 1) * Q_TILE == LEFT_CHUNKS * CHUNK
    assert att_width == conv_width
    head_dim = att_width // N_HEADS
    col_scale = jnp.where(jnp.arange(w_in.shape[2]) < att_width,
                          head_dim ** -0.5 * LOG2_E, 1.0).astype(F32)

    h = x.reshape(m, d)
    for i in range(depth):
        last = i == depth - 1
        proj = _in_projection(h, norm_in_g[i], (w_in[i] * col_scale).astype(BF16),
                              tm=min(PROJ_TILE[0], m), tn=min(PROJ_TILE[1], att_width),
                              group=att_width)
        y_a = _attention(proj, _bias_rows(rel_table[i]), attn_out_g[i],
                         batch=batch, seq=seq, width=att_width, n_heads=N_HEADS)
        y_c = _conv_module(proj, w_dw[i], b_dw[i], conv_ln_g[i], conv_ln_b[i],
                           w_pw[i].astype(BF16), b_pw[i], conv_out_g[i],
                           batch=batch, seq=seq, width=conv_width, col0=4, tt=min(CONV_ROWS, seq))
        w_pair = _cast_pair(w_out[i], w_ple_gate[i], rows=min(CAST_ROWS, d))
        h = _out_block(h, y_a, y_c, w_pair, ple_norm_g[i], b_ple_gate[i], p[i].reshape(m, -1),
                       w_ple[i].astype(BF16), final_g, tm=min(OUT_TILE[0], m),
                       tn=min(OUT_TILE[1], d), apply_final=last)
    return h.reshape(batch, seq, d)
```

```python
import functools

import numpy as np
import jax
import jax.numpy as jnp
from jax import lax
from jax.experimental import pallas as pl
from jax.experimental.pallas import tpu as pltpu

CHUNK = 64
LEFT_CHUNKS = 8
N_HEADS = 16
MAX_REL = 256
CONV_K = 31
EPS = 1e-6
NEG_INF = -1e30
LOG2_E = 1.4426950408889634

LANES = 128
Q_TILE = 256
KV_BLOCKS = 3
BAND_KEYS = KV_BLOCKS * Q_TILE
BIAS_ROW = 1024
HALO = 32
ROW_CHUNK = 32
VMEM_LIMIT = 60 * 1024 * 1024

F32 = jnp.float32
BF16 = jnp.bfloat16


def _cparams(n_axes):
    return pltpu.CompilerParams(
        dimension_semantics=("arbitrary",) * n_axes, vmem_limit_bytes=VMEM_LIMIT)


def _row_loop(n_rows, body, unroll=1):
    def step(r, carry):
        body(pl.ds(pl.multiple_of(r * ROW_CHUNK, ROW_CHUNK), ROW_CHUNK))
        return carry
    lax.fori_loop(0, n_rows // ROW_CHUNK, step, 0, unroll=unroll)


def _sigmoid(z):
    return 0.5 * jnp.tanh(0.5 * z) + 0.5


def _silu(z):
    h = 0.5 * z
    return h * (jnp.tanh(h) + 1.0)


def _proj_kernel(x_hbm, g_ref, w_ref, o_ref, x_buf, xn_ref, rinv_ref, sem):
    i = pl.program_id(0)
    j = pl.program_id(1)
    tm = x_buf.shape[0]

    def x_copy(block):
        rows = pl.ds(pl.multiple_of(block * tm, tm), tm)
        return pltpu.make_async_copy(x_hbm.at[rows, :], x_buf, sem)

    @pl.when((i == 0) & (j == 0))
    def _():
        x_copy(0).start()

    @pl.when(j == 0)
    def _():
        x_copy(i).wait()

        def norm_rows(rows):
            x = x_buf[rows, :]
            rinv_ref[rows, :] = lax.rsqrt(jnp.mean(x * x, axis=-1, keepdims=True) + EPS)
            xn_ref[rows, :] = (x * g_ref[...]).astype(BF16)
        _row_loop(tm, norm_rows, unroll=2)

    @pl.when((j == 1) & (i + 1 < pl.num_programs(0)))
    def _():
        x_copy(i + 1).start()

    acc = jnp.dot(xn_ref[...], w_ref[...], preferred_element_type=F32)
    o_ref[...] = (acc * rinv_ref[...]).astype(o_ref.dtype)


def _in_projection(x2, g, w_bf, *, tm, tn):
    m, d = x2.shape
    n = w_bf.shape[1]
    assert n // tn >= 2
    return pl.pallas_call(
        _proj_kernel,
        out_shape=jax.ShapeDtypeStruct((m, n), BF16),
        grid=(m // tm, n // tn),
        in_specs=[
            pl.BlockSpec(memory_space=pl.ANY),
            pl.BlockSpec((1, d), lambda i, j: (0, 0)),
            pl.BlockSpec((d, tn), lambda i, j: (0, j)),
        ],
        out_specs=pl.BlockSpec((tm, tn), lambda i, j: (i, j)),
        scratch_shapes=[pltpu.VMEM((tm, d), F32), pltpu.VMEM((tm, d), BF16),
                        pltpu.VMEM((tm, 1), F32), pltpu.SemaphoreType.DMA(())],
        compiler_params=_cparams(2),
        name="in_projection",
    )(x2, g.reshape(1, d), w_bf)


def _attn_kernel(q_ref, k0_ref, k1_ref, k2_ref, v0_ref, v1_ref, v2_ref, z_ref, brow_ref, g_ref,
                 o_ref, y_ref, bias_ref, vext_ref, *, n_heads, head_dim):
    t = pl.program_id(1)
    k_refs = (k0_ref, k1_ref, k2_ref)
    v_refs = (v0_ref, v1_ref, v2_ref)

    @pl.when((pl.program_id(0) == 0) & (t == 0))
    def _():
        qi = lax.broadcasted_iota(jnp.int32, (Q_TILE, BAND_KEYS), 0) // CHUNK
        kj = lax.broadcasted_iota(jnp.int32, (Q_TILE, BAND_KEYS), 1) // CHUNK
        in_band = (kj >= qi) & (kj <= qi + LEFT_CHUNKS)
        for h in range(n_heads):
            row = jnp.broadcast_to(brow_ref[h:h + 1, :], (Q_TILE, BIAS_ROW))
            skew = pltpu.roll(row, 0, 1, stride=1, stride_axis=0)
            bias_ref[h, :, 0:BAND_KEYS] = jnp.where(in_band, skew[:, :BAND_KEYS], NEG_INF)
            bias_ref[h, :, BAND_KEYS:] = jnp.full((Q_TILE, Q_TILE), NEG_INF, F32)
        vext_ref[:, :, head_dim:] = jnp.ones((n_heads, BAND_KEYS, head_dim), BF16)

    ss = jnp.zeros((Q_TILE, 1), F32)
    for h in range(n_heads):
        cols = slice(h * head_dim, (h + 1) * head_dim)
        q = q_ref[:, cols]
        s = []
        for j in range(KV_BLOCKS):
            vext_ref[h, j * Q_TILE:(j + 1) * Q_TILE, 0:head_dim] = v_refs[j][:, cols]
            sj = lax.dot_general(q, k_refs[j][:, cols], (((1,), (1,)), ((), ())),
                                 preferred_element_type=F32)
            if j < KV_BLOCKS - 1:
                valid = t + (j - (KV_BLOCKS - 1)) >= 0
                off = pl.multiple_of(jnp.where(valid, j * Q_TILE, BAND_KEYS), Q_TILE)
            else:
                off = j * Q_TILE
            s.append(sj + bias_ref[h, :, pl.ds(off, Q_TILE)])
        m = jnp.max(jnp.maximum(jnp.maximum(s[0], s[1]), s[2]), axis=-1, keepdims=True)
        oe = None
        for j in range(KV_BLOCKS):
            pj = jnp.exp2(s[j] - m).astype(BF16)
            part = jnp.dot(pj, vext_ref[h, j * Q_TILE:(j + 1) * Q_TILE, :],
                           preferred_element_type=F32)
            oe = part if oe is None else oe + part
        y = oe[:, 0:head_dim] / oe[:, head_dim:]
        ss = ss + jnp.sum(y * y, axis=-1, keepdims=True)
        y_ref[:, cols] = y * (g_ref[:, cols] * _silu(z_ref[:, cols].astype(F32)))

    rinv = lax.rsqrt(ss * (1.0 / (n_heads * head_dim)) + EPS)
    for r0 in range(0, Q_TILE, ROW_CHUNK):
        rows = slice(r0, r0 + ROW_CHUNK)
        o_ref[rows, :] = (y_ref[rows, :] * rinv[rows, :]).astype(o_ref.dtype)


def _bias_rows(table):
    e = np.arange(BIAS_ROW)
    e = np.where(e >= BAND_KEYS, e - BIAS_ROW, e)
    idx = np.clip((KV_BLOCKS - 1) * Q_TILE - e, -MAX_REL, MAX_REL) + MAX_REL
    return table[:, idx].astype(F32) * LOG2_E


def _attention(proj, brow, g, *, batch, seq, width, n_heads):
    m = proj.shape[0]
    tiles = seq // Q_TILE
    head_dim = width // n_heads

    def spec(off, col):
        return pl.BlockSpec((Q_TILE, width),
                            lambda b, t: (b * tiles + jnp.maximum(t + off, 0), col))

    in_specs = [spec(0, 0),
                spec(-2, 1), spec(-1, 1), spec(0, 1),
                spec(-2, 2), spec(-1, 2), spec(0, 2),
                spec(0, 3),
                pl.BlockSpec(brow.shape, lambda b, t: (0, 0)),
                pl.BlockSpec((1, width), lambda b, t: (0, 0))]
    return pl.pallas_call(
        functools.partial(_attn_kernel, n_heads=n_heads, head_dim=head_dim),
        out_shape=jax.ShapeDtypeStruct((m, width), BF16),
        grid=(batch, tiles),
        in_specs=in_specs,
        out_specs=pl.BlockSpec((Q_TILE, width), lambda b, t: (b * tiles + t, 0)),
        scratch_shapes=[pltpu.VMEM((Q_TILE, width), F32),
                        pltpu.VMEM((n_heads, Q_TILE, BAND_KEYS + Q_TILE), F32),
                        pltpu.VMEM((n_heads, BAND_KEYS, 2 * head_dim), BF16)],
        compiler_params=_cparams(2),
        name="band_attention",
    )(*([proj] * 8), brow, g.reshape(1, width))


def _conv_kernel(a_ref, gl_ref, z_ref, wdw_ref, bdw_ref, lng_ref, lnb_ref, wpw_ref, bpw_ref, g_ref,
                 o_ref, u2_ref, c_ref, yb_ref, pw_ref, *, strip_rows):
    tt, width = a_ref.shape
    n_slabs = width // LANES
    t = pl.program_id(1)

    @pl.when(t == 0)
    def _():
        u2_ref[...] = jnp.zeros_like(u2_ref)

    @pl.when(t > 0)
    def _():
        u2_ref[:, 0:2 * HALO, :] = u2_ref[:, 2 * tt:2 * (tt + HALO), :]

    def glu_rows(rows):
        u = a_ref[rows, :].astype(F32) * _sigmoid(gl_ref[rows, :].astype(F32))
        for c in range(n_slabs):
            u2_ref[c, pl.ds(2 * (HALO + rows.start), ROW_CHUNK, stride=2), :] = (
                u[:, c * LANES:(c + 1) * LANES])
    _row_loop(tt, glu_rows)

    def conv_slab(c, carry):
        cols = pl.ds(pl.multiple_of(c * LANES, LANES), LANES)
        for r0 in range(0, tt, strip_rows):
            acc = jnp.zeros((strip_rows, LANES), F32)
            for k in range(CONV_K):
                first = r0 + HALO - (CONV_K - 1) + k
                acc += u2_ref[c, pl.ds(2 * first, strip_rows, stride=2), :] * wdw_ref[k:k + 1, cols]
            c_ref[r0:r0 + strip_rows, cols] = acc + bdw_ref[:, cols]
        return carry
    lax.fori_loop(0, n_slabs, conv_slab, 0)

    def norm_rows(rows):
        c = c_ref[rows, :]
        mu = jnp.mean(c, axis=-1, keepdims=True)
        cc = c - mu
        var = jnp.mean(cc * cc, axis=-1, keepdims=True)
        y = cc * lax.rsqrt(var + EPS) * lng_ref[...] + lnb_ref[...]
        yb_ref[rows, :] = _silu(y).astype(BF16)
    _row_loop(tt, norm_rows, unroll=4)

    pw_ref[...] = jnp.dot(yb_ref[...], wpw_ref[...], preferred_element_type=F32)

    def finish_rows(rows):
        pw = pw_ref[rows, :] + bpw_ref[...]
        ms = jnp.mean(pw * pw, axis=-1, keepdims=True)
        z = z_ref[rows, :].astype(F32)
        o_ref[rows, :] = (pw * lax.rsqrt(ms + EPS) * g_ref[...] * _silu(z)).astype(o_ref.dtype)
    _row_loop(tt, finish_rows, unroll=2)


def _conv_module(proj, w_dw, b_dw, ln_g, ln_b, w_pw_bf, b_pw, g, *, batch, seq, width, col0, tt):
    m = proj.shape[0]
    tiles = seq // tt
    w_dw_p = jnp.zeros((HALO, width), F32).at[:CONV_K].set(w_dw)

    def spec(col):
        return pl.BlockSpec((tt, width), lambda b, t: (b * tiles + t, col))

    def vec():
        return pl.BlockSpec((1, width), lambda b, t: (0, 0))

    return pl.pallas_call(
        functools.partial(_conv_kernel, strip_rows=128),
        out_shape=jax.ShapeDtypeStruct((m, width), BF16),
        grid=(batch, tiles),
        in_specs=[spec(col0), spec(col0 + 1), spec(col0 + 2),
                  pl.BlockSpec((HALO, width), lambda b, t: (0, 0)),
                  vec(), vec(), vec(),
                  pl.BlockSpec((width, width), lambda b, t: (0, 0)),
                  vec(), vec()],
        out_specs=pl.BlockSpec((tt, width), lambda b, t: (b * tiles + t, 0)),
        scratch_shapes=[pltpu.VMEM((width // LANES, 2 * (tt + HALO), LANES), F32),
                        pltpu.VMEM((tt, width), F32),
                        pltpu.VMEM((tt, width), BF16),
                        pltpu.VMEM((tt, width), F32)],
        compiler_params=_cparams(2),
        name="conv_module",
    )(proj, proj, proj, w_dw_p, b_dw.reshape(1, width), ln_g.reshape(1, width),
      ln_b.reshape(1, width), w_pw_bf, b_pw.reshape(1, width), g.reshape(1, width))


def _cast_pair_kernel(a_ref, b_ref, o_ref):
    @pl.when(pl.program_id(0) == 0)
    def _():
        o_ref[...] = a_ref[...].astype(o_ref.dtype)

    @pl.when(pl.program_id(0) == 1)
    def _():
        o_ref[...] = b_ref[...].astype(o_ref.dtype)


def _cast_pair(a, b, *, rows):
    r, c = a.shape
    nr = r // rows
    return pl.pallas_call(
        _cast_pair_kernel,
        out_shape=jax.ShapeDtypeStruct((2, r, c), BF16),
        grid=(2, nr),
        in_specs=[pl.BlockSpec((rows, c), lambda s, t: (jnp.where(s == 0, t, nr - 1), 0)),
                  pl.BlockSpec((rows, c), lambda s, t: (jnp.where(s == 1, t, 0), 0))],
        out_specs=pl.BlockSpec((None, rows, c), lambda s, t: (s, t, 0)),
        compiler_params=_cparams(2),
        name="cast_weight_pair",
    )(a, b)


def _out_kernel(x_ref, ya_ref, yc_ref, w_ref, pg_ref, bg_ref, p_ref, wp_ref, fg_ref,
                o_ref, h_ref, hn_ref, ssh_ref, sso_ref, *, nj, tn, n_blocks, apply_final):
    i = pl.program_id(0)
    j = pl.program_id(1)
    cur = i % 2
    prev = 1 - cur
    wa = ya_ref.shape[1]
    d = hn_ref.shape[1]
    to = o_ref.shape[1]

    @pl.when((i < n_blocks) & (j == 0))
    def _():
        ssh_ref[...] = jnp.zeros_like(ssh_ref)
        sso_ref[cur] = jnp.zeros(sso_ref.shape[1:], F32)

    @pl.when((i < n_blocks) & (j < nj))
    def _():
        cols = pl.ds(pl.multiple_of(j * tn, tn), tn)
        acc = jnp.dot(ya_ref[...], w_ref[0:wa, :], preferred_element_type=F32)
        acc += jnp.dot(yc_ref[...], w_ref[wa:, :], preferred_element_type=F32)
        h = x_ref[...] + acc
        h_ref[cur, :, cols] = h
        hn_ref[:, cols] = (h * pg_ref[:, cols]).astype(BF16)
        ssh_ref[...] += jnp.sum(h * h, axis=-1, keepdims=True)

    @pl.when((i < n_blocks) & (j >= nj))
    def _():
        cols = pl.ds(pl.multiple_of((j - nj) * tn, tn), tn)
        rinv = lax.rsqrt(ssh_ref[...] * (1.0 / d) + EPS)
        gate = _sigmoid(
            jnp.dot(hn_ref[...], w_ref[...], preferred_element_type=F32) * rinv + bg_ref[...])
        pe = jnp.dot(p_ref[...].astype(BF16), wp_ref[...], preferred_element_type=F32)
        h2 = h_ref[cur, :, cols] + gate * pe
        h_ref[cur, :, cols] = h2
        sso_ref[cur] += jnp.sum(h2 * h2, axis=-1, keepdims=True)

    @pl.when(i > 0)
    def _():
        ocols = pl.ds(pl.multiple_of(j * to, to), to)
        y = h_ref[prev, :, ocols]
        if apply_final:
            y = y * lax.rsqrt(sso_ref[prev] * (1.0 / d) + EPS) * fg_ref[:, ocols]
        o_ref[...] = y


def _out_block(x2, y_a, y_c, w_pair_bf, ple_g, b_gate, p2, w_ple_bf, final_g, *, tm, tn, apply_final):
    m, d = x2.shape
    wa = y_a.shape[1]
    wc = y_c.shape[1]
    pd = p2.shape[1]
    nj = d // tn
    nb = m // tm
    to = d // (2 * nj)

    def row(i):
        return jnp.minimum(i, nb - 1)

    def first(j):
        return jnp.minimum(j, nj - 1)

    def second(j):
        return jnp.maximum(j - nj, 0)

    def w_block(i, j):
        jj = jnp.where(i < nb, j, 2 * nj - 1)
        return (jj // nj, 0, jj % nj)

    return pl.pallas_call(
        functools.partial(_out_kernel, nj=nj, tn=tn, n_blocks=nb, apply_final=apply_final),
        out_shape=jax.ShapeDtypeStruct((m, d), F32),
        grid=(nb + 1, 2 * nj),
        in_specs=[
            pl.BlockSpec((tm, tn), lambda i, j: (row(i), first(j))),
            pl.BlockSpec((tm, wa), lambda i, j: (row(i), 0)),
            pl.BlockSpec((tm, wc), lambda i, j: (row(i), 0)),
            pl.BlockSpec((None, d, tn), w_block),
            pl.BlockSpec((1, d), lambda i, j: (0, 0)),
            pl.BlockSpec((1, tn), lambda i, j: (0, second(j))),
            pl.BlockSpec((tm, pd), lambda i, j: (row(i), 0)),
            pl.BlockSpec((pd, tn), lambda i, j: (0, second(j))),
            pl.BlockSpec((1, d), lambda i, j: (0, 0)),
        ],
        out_specs=pl.BlockSpec((tm, to),
                               lambda i, j: (jnp.maximum(i - 1, 0), jnp.where(i == 0, 0, j))),
        scratch_shapes=[pltpu.VMEM((2, tm, d), F32), pltpu.VMEM((tm, d), BF16),
                        pltpu.VMEM((tm, 1), F32), pltpu.VMEM((2, tm, 1), F32)],
        compiler_params=_cparams(2),
        name="out_block",
    )(x2, y_a, y_c, w_pair_bf, ple_g.reshape(1, d), b_gate.reshape(1, d), p2, w_ple_bf,
      final_g.reshape(1, d))


def kernel(x, p, norm_in_g, w_in, rel_table, w_dw, b_dw, conv_ln_g, conv_ln_b, w_pw, b_pw,
           attn_out_g, conv_out_g, w_out, ple_norm_g, w_ple_gate, b_ple_gate, w_ple, final_g):
    batch, seq, d = x.shape
    depth = w_in.shape[0]
    conv_width = w_pw.shape[1]
    att_width = d - conv_width
    m = batch * seq
    assert depth >= 1
    assert seq % Q_TILE == 0 and Q_TILE % CHUNK == 0
    assert (KV_BLOCKS - 1) * Q_TILE == LEFT_CHUNKS * CHUNK
    assert att_width == conv_width
    head_dim = att_width // N_HEADS
    col_scale = jnp.where(jnp.arange(w_in.shape[2]) < att_width,
                          head_dim ** -0.5 * LOG2_E, 1.0).astype(F32)

    h = x.reshape(m, d)
    for i in range(depth):
        last = i == depth - 1
        proj = _in_projection(h, norm_in_g[i], (w_in[i] * col_scale).astype(BF16),
                              tm=min(1024, m), tn=min(1024, d))
        y_a = _attention(proj, _bias_rows(rel_table[i]), attn_out_g[i],
                         batch=batch, seq=seq, width=att_width, n_heads=N_HEADS)
        y_c = _conv_module(proj, w_dw[i], b_dw[i], conv_ln_g[i], conv_ln_b[i],
                           w_pw[i].astype(BF16), b_pw[i], conv_out_g[i],
                           batch=batch, seq=seq, width=conv_width, col0=4, tt=min(256, seq))
        w_pair = _cast_pair(w_out[i], w_ple_gate[i], rows=256)
        h = _out_block(h, y_a, y_c, w_pair, ple_norm_g[i], b_ple_gate[i], p[i].reshape(m, -1),
                       w_ple[i].astype(BF16), final_g, tm=min(512, m), tn=min(1024, d),
                       apply_final=last)
    return h.reshape(batch, seq, d)
```

```python
import functools

import numpy as np
import jax
import jax.numpy as jnp
from jax import lax
from jax.experimental import pallas as pl
from jax.experimental.pallas import tpu as pltpu

CHUNK = 64
LEFT_CHUNKS = 8
N_HEADS = 16
MAX_REL = 256
CONV_K = 31
EPS = 1e-6
NEG_INF = -1e30
LOG2_E = 1.4426950408889634

LANES = 128
Q_TILE = 256
KV_BLOCKS = 3
BAND_KEYS = KV_BLOCKS * Q_TILE
BIAS_ROW = 1024
HALO = 32
ROW_CHUNK = 32
VMEM_LIMIT = 60 * 1024 * 1024

F32 = jnp.float32
BF16 = jnp.bfloat16


def _cparams(n_axes):
    return pltpu.CompilerParams(
        dimension_semantics=("arbitrary",) * n_axes, vmem_limit_bytes=VMEM_LIMIT)


def _row_loop(n_rows, body, unroll=1):
    def step(r, carry):
        body(pl.ds(pl.multiple_of(r * ROW_CHUNK, ROW_CHUNK), ROW_CHUNK))
        return carry
    lax.fori_loop(0, n_rows // ROW_CHUNK, step, 0, unroll=unroll)


def _sigmoid(z):
    return 0.5 * jnp.tanh(0.5 * z) + 0.5


def _silu(z):
    h = 0.5 * z
    return h * (jnp.tanh(h) + 1.0)


def _proj_kernel(x_hbm, g_ref, w_ref, o_ref, x_buf, xn_ref, rinv_ref, sem):
    i = pl.program_id(0)
    j = pl.program_id(1)
    tm = x_buf.shape[0]

    def x_copy(block):
        rows = pl.ds(pl.multiple_of(block * tm, tm), tm)
        return pltpu.make_async_copy(x_hbm.at[rows, :], x_buf, sem)

    @pl.when((i == 0) & (j == 0))
    def _():
        x_copy(0).start()

    @pl.when(j == 0)
    def _():
        x_copy(i).wait()

        def norm_rows(rows):
            x = x_buf[rows, :]
            rinv_ref[rows, :] = lax.rsqrt(jnp.mean(x * x, axis=-1, keepdims=True) + EPS)
            xn_ref[rows, :] = (x * g_ref[...]).astype(BF16)
        _row_loop(tm, norm_rows, unroll=2)

    @pl.when((j == 1) & (i + 1 < pl.num_programs(0)))
    def _():
        x_copy(i + 1).start()

    acc = jnp.dot(xn_ref[...], w_ref[...], preferred_element_type=F32)
    o_ref[...] = (acc * rinv_ref[...]).astype(o_ref.dtype)


def _in_projection(x2, g, w_bf, *, tm, tn):
    m, d = x2.shape
    n = w_bf.shape[1]
    assert n // tn >= 2
    return pl.pallas_call(
        _proj_kernel,
        out_shape=jax.ShapeDtypeStruct((m, n), BF16),
        grid=(m // tm, n // tn),
        in_specs=[
            pl.BlockSpec(memory_space=pl.ANY),
            pl.BlockSpec((1, d), lambda i, j: (0, 0)),
            pl.BlockSpec((d, tn), lambda i, j: (0, j)),
        ],
        out_specs=pl.BlockSpec((tm, tn), lambda i, j: (i, j)),
        scratch_shapes=[pltpu.VMEM((tm, d), F32), pltpu.VMEM((tm, d), BF16),
                        pltpu.VMEM((tm, 1), F32), pltpu.SemaphoreType.DMA(())],
        compiler_params=_cparams(2),
        name="in_projection",
    )(x2, g.reshape(1, d), w_bf)


def _attn_kernel(q_ref, k0_ref, k1_ref, k2_ref, v0_ref, v1_ref, v2_ref, z_ref, brow_ref, g_ref,
                 o_ref, y_ref, bias_ref, vext_ref, *, n_heads, head_dim):
    t = pl.program_id(1)
    k_refs = (k0_ref, k1_ref, k2_ref)
    v_refs = (v0_ref, v1_ref, v2_ref)

    @pl.when((pl.program_id(0) == 0) & (t == 0))
    def _():
        qi = lax.broadcasted_iota(jnp.int32, (Q_TILE, BAND_KEYS), 0) // CHUNK
        kj = lax.broadcasted_iota(jnp.int32, (Q_TILE, BAND_KEYS), 1) // CHUNK
        in_band = (kj >= qi) & (kj <= qi + LEFT_CHUNKS)
        for h in range(n_heads):
            row = jnp.broadcast_to(brow_ref[h:h + 1, :], (Q_TILE, BIAS_ROW))
            skew = pltpu.roll(row, 0, 1, stride=1, stride_axis=0)
            bias_ref[h, :, 0:BAND_KEYS] = jnp.where(in_band, skew[:, :BAND_KEYS], NEG_INF)
            bias_ref[h, :, BAND_KEYS:] = jnp.full((Q_TILE, Q_TILE), NEG_INF, F32)
        vext_ref[:, :, head_dim:] = jnp.ones((n_heads, BAND_KEYS, head_dim), BF16)

    ss = jnp.zeros((Q_TILE, 1), F32)
    for h in range(n_heads):
        cols = slice(h * head_dim, (h + 1) * head_dim)
        q = q_ref[:, cols]
        s = []
        for j in range(KV_BLOCKS):
            vext_ref[h, j * Q_TILE:(j + 1) * Q_TILE, 0:head_dim] = v_refs[j][:, cols]
            sj = lax.dot_general(q, k_refs[j][:, cols], (((1,), (1,)), ((), ())),
                                 preferred_element_type=F32)
            if j < KV_BLOCKS - 1:
                valid = t + (j - (KV_BLOCKS - 1)) >= 0
                off = pl.multiple_of(jnp.where(valid, j * Q_TILE, BAND_KEYS), Q_TILE)
            else:
                off = j * Q_TILE
            s.append(sj + bias_ref[h, :, pl.ds(off, Q_TILE)])
        m = jnp.max(jnp.maximum(jnp.maximum(s[0], s[1]), s[2]), axis=-1, keepdims=True)
        oe = None
        for j in range(KV_BLOCKS):
            pj = jnp.exp2(s[j] - m).astype(BF16)
            part = jnp.dot(pj, vext_ref[h, j * Q_TILE:(j + 1) * Q_TILE, :],
                           preferred_element_type=F32)
            oe = part if oe is None else oe + part
        y = oe[:, 0:head_dim] / oe[:, head_dim:]
        ss = ss + jnp.sum(y * y, axis=-1, keepdims=True)
        y_ref[:, cols] = y * (g_ref[:, cols] * _silu(z_ref[:, cols].astype(F32)))

    rinv = lax.rsqrt(ss * (1.0 / (n_heads * head_dim)) + EPS)
    for r0 in range(0, Q_TILE, ROW_CHUNK):
        rows = slice(r0, r0 + ROW_CHUNK)
        o_ref[rows, :] = (y_ref[rows, :] * rinv[rows, :]).astype(o_ref.dtype)


def _bias_rows(table):
    e = np.arange(BIAS_ROW)
    e = np.where(e >= BAND_KEYS, e - BIAS_ROW, e)
    idx = np.clip((KV_BLOCKS - 1) * Q_TILE - e, -MAX_REL, MAX_REL) + MAX_REL
    return table[:, idx].astype(F32) * LOG2_E


def _attention(proj, brow, g, *, batch, seq, width, n_heads):
    m = proj.shape[0]
    tiles = seq // Q_TILE
    head_dim = width // n_heads

    def spec(off, col):
        return pl.BlockSpec((Q_TILE, width),
                            lambda b, t: (b * tiles + jnp.maximum(t + off, 0), col))

    in_specs = [spec(0, 0),
                spec(-2, 1), spec(-1, 1), spec(0, 1),
                spec(-2, 2), spec(-1, 2), spec(0, 2),
                spec(0, 3),
                pl.BlockSpec(brow.shape, lambda b, t: (0, 0)),
                pl.BlockSpec((1, width), lambda b, t: (0, 0))]
    return pl.pallas_call(
        functools.partial(_attn_kernel, n_heads=n_heads, head_dim=head_dim),
        out_shape=jax.ShapeDtypeStruct((m, width), BF16),
        grid=(batch, tiles),
        in_specs=in_specs,
        out_specs=pl.BlockSpec((Q_TILE, width), lambda b, t: (b * tiles + t, 0)),
        scratch_shapes=[pltpu.VMEM((Q_TILE, width), F32),
                        pltpu.VMEM((n_heads, Q_TILE, BAND_KEYS + Q_TILE), F32),
                        pltpu.VMEM((n_heads, BAND_KEYS, 2 * head_dim), BF16)],
        compiler_params=_cparams(2),
        name="band_attention",
    )(*([proj] * 8), brow, g.reshape(1, width))


def _conv_kernel(a_ref, gl_ref, z_ref, wdw_ref, bdw_ref, lng_ref, lnb_ref, wpw_ref, bpw_ref, g_ref,
                 o_ref, u2_ref, c_ref, yb_ref, pw_ref, *, strip_rows):
    tt, width = a_ref.shape
    n_slabs = width // LANES
    t = pl.program_id(1)

    @pl.when(t == 0)
    def _():
        u2_ref[...] = jnp.zeros_like(u2_ref)

    @pl.when(t > 0)
    def _():
        u2_ref[:, 0:2 * HALO, :] = u2_ref[:, 2 * tt:2 * (tt + HALO), :]

    def glu_rows(rows):
        u = a_ref[rows, :].astype(F32) * _sigmoid(gl_ref[rows, :].astype(F32))
        for c in range(n_slabs):
            u2_ref[c, pl.ds(2 * (HALO + rows.start), ROW_CHUNK, stride=2), :] = (
                u[:, c * LANES:(c + 1) * LANES])
    _row_loop(tt, glu_rows)

    def conv_slab(c, carry):
        cols = pl.ds(pl.multiple_of(c * LANES, LANES), LANES)
        for r0 in range(0, tt, strip_rows):
            acc = jnp.zeros((strip_rows, LANES), F32)
            for k in range(CONV_K):
                first = r0 + HALO - (CONV_K - 1) + k
                acc += u2_ref[c, pl.ds(2 * first, strip_rows, stride=2), :] * wdw_ref[k:k + 1, cols]
            c_ref[r0:r0 + strip_rows, cols] = acc + bdw_ref[:, cols]
        return carry
    lax.fori_loop(0, n_slabs, conv_slab, 0)

    def norm_rows(rows):
        c = c_ref[rows, :]
        mu = jnp.mean(c, axis=-1, keepdims=True)
        cc = c - mu
        var = jnp.mean(cc * cc, axis=-1, keepdims=True)
        h = cc * lax.rsqrt(var + EPS) * (0.5 * lng_ref[...]) + 0.5 * lnb_ref[...]
        yb_ref[rows, :] = (h * (jnp.tanh(h) + 1.0)).astype(BF16)
    _row_loop(tt, norm_rows, unroll=4)

    pw_ref[...] = jnp.dot(yb_ref[...], wpw_ref[...], preferred_element_type=F32)

    def finish_rows(rows):
        pw = pw_ref[rows, :] + bpw_ref[...]
        ms = jnp.mean(pw * pw, axis=-1, keepdims=True)
        z = z_ref[rows, :].astype(F32)
        o_ref[rows, :] = (pw * lax.rsqrt(ms + EPS) * g_ref[...] * _silu(z)).astype(o_ref.dtype)
    _row_loop(tt, finish_rows, unroll=2)


def _conv_module(proj, w_dw, b_dw, ln_g, ln_b, w_pw_bf, b_pw, g, *, batch, seq, width, col0, tt):
    m = proj.shape[0]
    tiles = seq // tt
    w_dw_p = jnp.zeros((HALO, width), F32).at[:CONV_K].set(w_dw)

    def spec(col):
        return pl.BlockSpec((tt, width), lambda b, t: (b * tiles + t, col))

    def vec():
        return pl.BlockSpec((1, width), lambda b, t: (0, 0))

    return pl.pallas_call(
        functools.partial(_conv_kernel, strip_rows=128),
        out_shape=jax.ShapeDtypeStruct((m, width), BF16),
        grid=(batch, tiles),
        in_specs=[spec(col0), spec(col0 + 1), spec(col0 + 2),
                  pl.BlockSpec((HALO, width), lambda b, t: (0, 0)),
                  vec(), vec(), vec(),
                  pl.BlockSpec((width, width), lambda b, t: (0, 0)),
                  vec(), vec()],
        out_specs=pl.BlockSpec((tt, width), lambda b, t: (b * tiles + t, 0)),
        scratch_shapes=[pltpu.VMEM((width // LANES, 2 * (tt + HALO), LANES), F32),
                        pltpu.VMEM((tt, width), F32),
                        pltpu.VMEM((tt, width), BF16),
                        pltpu.VMEM((tt, width), F32)],
        compiler_params=_cparams(2),
        name="conv_module",
    )(proj, proj, proj, w_dw_p, b_dw.reshape(1, width), ln_g.reshape(1, width),
      ln_b.reshape(1, width), w_pw_bf, b_pw.reshape(1, width), g.reshape(1, width))


def _cast_pair_kernel(a_ref, b_ref, o_ref):
    @pl.when(pl.program_id(0) == 0)
    def _():
        o_ref[...] = a_ref[...].astype(o_ref.dtype)

    @pl.when(pl.program_id(0) == 1)
    def _():
        o_ref[...] = b_ref[...].astype(o_ref.dtype)


def _cast_pair(a, b, *, rows):
    r, c = a.shape
    nr = r // rows
    return pl.pallas_call(
        _cast_pair_kernel,
        out_shape=jax.ShapeDtypeStruct((2, r, c), BF16),
        grid=(2, nr),
        in_specs=[pl.BlockSpec((rows, c), lambda s, t: (jnp.where(s == 0, t, nr - 1), 0)),
                  pl.BlockSpec((rows, c), lambda s, t: (jnp.where(s == 1, t, 0), 0))],
        out_specs=pl.BlockSpec((None, rows, c), lambda s, t: (s, t, 0)),
        compiler_params=_cparams(2),
        name="cast_weight_pair",
    )(a, b)


def _out_kernel(x_ref, ya_ref, yc_ref, w_ref, pg_ref, bg_ref, p_ref, wp_ref, fg_ref,
                o_ref, h_ref, hn_ref, ssh_ref, sso_ref, *, nj, tn, n_blocks, apply_final):
    i = pl.program_id(0)
    j = pl.program_id(1)
    cur = i % 2
    prev = 1 - cur
    wa = ya_ref.shape[1]
    d = hn_ref.shape[1]
    to = o_ref.shape[1]

    @pl.when((i < n_blocks) & (j == 0))
    def _():
        ssh_ref[...] = jnp.zeros_like(ssh_ref)
        sso_ref[cur] = jnp.zeros(sso_ref.shape[1:], F32)

    @pl.when((i < n_blocks) & (j < nj))
    def _():
        cols = pl.ds(pl.multiple_of(j * tn, tn), tn)
        acc = jnp.dot(ya_ref[...], w_ref[0:wa, :], preferred_element_type=F32)
        acc += jnp.dot(yc_ref[...], w_ref[wa:, :], preferred_element_type=F32)
        h = x_ref[...] + acc
        h_ref[cur, :, cols] = h
        hn_ref[:, cols] = (h * pg_ref[:, cols]).astype(BF16)
        ssh_ref[...] += jnp.sum(h * h, axis=-1, keepdims=True)

    @pl.when((i < n_blocks) & (j >= nj))
    def _():
        cols = pl.ds(pl.multiple_of((j - nj) * tn, tn), tn)
        half_rinv = 0.5 * lax.rsqrt(ssh_ref[...] * (1.0 / d) + EPS)
        gate2 = jnp.tanh(
            jnp.dot(hn_ref[...], w_ref[...], preferred_element_type=F32) * half_rinv
            + 0.5 * bg_ref[...]) + 1.0
        half_pe = jnp.dot((0.5 * p_ref[...]).astype(BF16), wp_ref[...], preferred_element_type=F32)
        h2 = h_ref[cur, :, cols] + gate2 * half_pe
        h_ref[cur, :, cols] = h2
        sso_ref[cur] += jnp.sum(h2 * h2, axis=-1, keepdims=True)

    @pl.when(i > 0)
    def _():
        ocols = pl.ds(pl.multiple_of(j * to, to), to)
        y = h_ref[prev, :, ocols]
        if apply_final:
            y = y * lax.rsqrt(sso_ref[prev] * (1.0 / d) + EPS) * fg_ref[:, ocols]
        o_ref[...] = y


def _out_block(x2, y_a, y_c, w_pair_bf, ple_g, b_gate, p2, w_ple_bf, final_g, *, tm, tn, apply_final):
    m, d = x2.shape
    wa = y_a.shape[1]
    wc = y_c.shape[1]
    pd = p2.shape[1]
    nj = d // tn
    nb = m // tm
    to = d // (2 * nj)

    def row(i):
        return jnp.minimum(i, nb - 1)

    def first(j):
        return jnp.minimum(j, nj - 1)

    def second(j):
        return jnp.maximum(j - nj, 0)

    def w_block(i, j):
        jj = jnp.where(i < nb, j, 2 * nj - 1)
        return (jj // nj, 0, jj % nj)

    return pl.pallas_call(
        functools.partial(_out_kernel, nj=nj, tn=tn, n_blocks=nb, apply_final=apply_final),
        out_shape=jax.ShapeDtypeStruct((m, d), F32),
        grid=(nb + 1, 2 * nj),
        in_specs=[
            pl.BlockSpec((tm, tn), lambda i, j: (row(i), first(j))),
            pl.BlockSpec((tm, wa), lambda i, j: (row(i), 0)),
            pl.BlockSpec((tm, wc), lambda i, j: (row(i), 0)),
            pl.BlockSpec((None, d, tn), w_block),
            pl.BlockSpec((1, d), lambda i, j: (0, 0)),
            pl.BlockSpec((1, tn), lambda i, j: (0, second(j))),
            pl.BlockSpec((tm, pd), lambda i, j: (row(i), 0)),
            pl.BlockSpec((pd, tn), lambda i, j: (0, second(j))),
            pl.BlockSpec((1, d), lambda i, j: (0, 0)),
        ],
        out_specs=pl.BlockSpec((tm, to),
                               lambda i, j: (jnp.maximum(i - 1, 0), jnp.where(i == 0, 0, j))),
        scratch_shapes=[pltpu.VMEM((2, tm, d), F32), pltpu.VMEM((tm, d), BF16),
                        pltpu.VMEM((tm, 1), F32), pltpu.VMEM((2, tm, 1), F32)],
        compiler_params=_cparams(2),
        name="out_block",
    )(x2, y_a, y_c, w_pair_bf, ple_g.reshape(1, d), b_gate.reshape(1, d), p2, w_ple_bf,
      final_g.reshape(1, d))


def kernel(x, p, norm_in_g, w_in, rel_table, w_dw, b_dw, conv_ln_g, conv_ln_b, w_pw, b_pw,
           attn_out_g, conv_out_g, w_out, ple_norm_g, w_ple_gate, b_ple_gate, w_ple, final_g):
    batch, seq, d = x.shape
    depth = w_in.shape[0]
    conv_width = w_pw.shape[1]
    att_width = d - conv_width
    m = batch * seq
    assert depth >= 1
    assert seq % Q_TILE == 0 and Q_TILE % CHUNK == 0
    assert (KV_BLOCKS - 1) * Q_TILE == LEFT_CHUNKS * CHUNK
    assert att_width == conv_width
    head_dim = att_width // N_HEADS
    col_scale = jnp.where(jnp.arange(w_in.shape[2]) < att_width,
                          head_dim ** -0.5 * LOG2_E, 1.0).astype(F32)

    h = x.reshape(m, d)
    for i in range(depth):
        last = i == depth - 1
        proj = _in_projection(h, norm_in_g[i], (w_in[i] * col_scale).astype(BF16),
                              tm=min(1024, m), tn=min(1024, d))
        y_a = _attention(proj, _bias_rows(rel_table[i]), attn_out_g[i],
                         batch=batch, seq=seq, width=att_width, n_heads=N_HEADS)
        y_c = _conv_module(proj, w_dw[i], b_dw[i], conv_ln_g[i], conv_ln_b[i],
                           w_pw[i].astype(BF16), b_pw[i], conv_out_g[i],
                           batch=batch, seq=seq, width=conv_width, col0=4, tt=min(256, seq))
        w_pair = _cast_pair(w_out[i], w_ple_gate[i], rows=256)
        h = _out_block(h, y_a, y_c, w_pair, ple_norm_g[i], b_ple_gate[i], p[i].reshape(m, -1),
                       w_ple[i].astype(BF16), final_g, tm=min(512, m), tn=min(1024, d),
                       apply_final=last)
    return h.reshape(batch, seq, d)
```

```python
import functools

import numpy as np
import jax
import jax.numpy as jnp
from jax import lax
from jax.experimental import pallas as pl
from jax.experimental.pallas import tpu as pltpu

CHUNK = 64
LEFT_CHUNKS = 8
N_HEADS = 16
MAX_REL = 256
CONV_K = 31
EPS = 1e-6
NEG_INF = -1e30
LOG2_E = 1.4426950408889634

LANES = 128
Q_TILE = 256
KV_BLOCKS = 3
BAND_KEYS = KV_BLOCKS * Q_TILE
BIAS_ROW = 1024
HALO = 32
ROW_CHUNK = 32
VMEM_LIMIT = 60 * 1024 * 1024

F32 = jnp.float32
BF16 = jnp.bfloat16


def _cparams(n_axes):
    return pltpu.CompilerParams(
        dimension_semantics=("arbitrary",) * n_axes, vmem_limit_bytes=VMEM_LIMIT)


def _row_loop(n_rows, body, unroll=1):
    def step(r, carry):
        body(pl.ds(pl.multiple_of(r * ROW_CHUNK, ROW_CHUNK), ROW_CHUNK))
        return carry
    lax.fori_loop(0, n_rows // ROW_CHUNK, step, 0, unroll=unroll)


def _sigmoid(z):
    return 0.5 * jnp.tanh(0.5 * z) + 0.5


def _silu(z):
    h = 0.5 * z
    return h * (jnp.tanh(h) + 1.0)


def _proj_kernel(x_hbm, g_ref, w_ref, o_ref, x_buf, xn_ref, rinv_ref, sem):
    i = pl.program_id(0)
    j = pl.program_id(1)
    tm = x_buf.shape[0]

    def x_copy(block):
        rows = pl.ds(pl.multiple_of(block * tm, tm), tm)
        return pltpu.make_async_copy(x_hbm.at[rows, :], x_buf, sem)

    @pl.when((i == 0) & (j == 0))
    def _():
        x_copy(0).start()

    @pl.when(j == 0)
    def _():
        x_copy(i).wait()

        def norm_rows(rows):
            x = x_buf[rows, :]
            rinv_ref[rows, :] = lax.rsqrt(jnp.mean(x * x, axis=-1, keepdims=True) + EPS)
            xn_ref[rows, :] = (x * g_ref[...]).astype(BF16)
        _row_loop(tm, norm_rows, unroll=2)

    @pl.when((j == 1) & (i + 1 < pl.num_programs(0)))
    def _():
        x_copy(i + 1).start()

    acc = jnp.dot(xn_ref[...], w_ref[...], preferred_element_type=F32)
    o_ref[...] = (acc * rinv_ref[...]).astype(o_ref.dtype)


def _in_projection(x2, g, w_bf, *, tm, tn):
    m, d = x2.shape
    n = w_bf.shape[1]
    assert n // tn >= 2
    return pl.pallas_call(
        _proj_kernel,
        out_shape=jax.ShapeDtypeStruct((m, n), BF16),
        grid=(m // tm, n // tn),
        in_specs=[
            pl.BlockSpec(memory_space=pl.ANY),
            pl.BlockSpec((1, d), lambda i, j: (0, 0)),
            pl.BlockSpec((d, tn), lambda i, j: (0, j)),
        ],
        out_specs=pl.BlockSpec((tm, tn), lambda i, j: (i, j)),
        scratch_shapes=[pltpu.VMEM((tm, d), F32), pltpu.VMEM((tm, d), BF16),
                        pltpu.VMEM((tm, 1), F32), pltpu.SemaphoreType.DMA(())],
        compiler_params=_cparams(2),
        name="in_projection",
    )(x2, g.reshape(1, d), w_bf)


def _attn_kernel(q_ref, k0_ref, k1_ref, k2_ref, v0_ref, v1_ref, v2_ref, z_ref, brow_ref, g_ref,
                 o_ref, y_ref, bias_ref, vext_ref, *, n_heads, head_dim):
    t = pl.program_id(1)
    k_refs = (k0_ref, k1_ref, k2_ref)
    v_refs = (v0_ref, v1_ref, v2_ref)

    @pl.when((pl.program_id(0) == 0) & (t == 0))
    def _():
        qi = lax.broadcasted_iota(jnp.int32, (Q_TILE, BAND_KEYS), 0) // CHUNK
        kj = lax.broadcasted_iota(jnp.int32, (Q_TILE, BAND_KEYS), 1) // CHUNK
        in_band = (kj >= qi) & (kj <= qi + LEFT_CHUNKS)
        for h in range(n_heads):
            row = jnp.broadcast_to(brow_ref[h:h + 1, :], (Q_TILE, BIAS_ROW))
            skew = pltpu.roll(row, 0, 1, stride=1, stride_axis=0)
            bias_ref[h, :, 0:BAND_KEYS] = jnp.where(in_band, skew[:, :BAND_KEYS], NEG_INF)
            bias_ref[h, :, BAND_KEYS:] = jnp.full((Q_TILE, Q_TILE), NEG_INF, F32)
        vext_ref[:, :, head_dim:] = jnp.ones((n_heads, BAND_KEYS, head_dim), BF16)

    ss = jnp.zeros((Q_TILE, 1), F32)
    for h in range(n_heads):
        cols = slice(h * head_dim, (h + 1) * head_dim)
        q = q_ref[:, cols]
        s = []
        for j in range(KV_BLOCKS):
            vext_ref[h, j * Q_TILE:(j + 1) * Q_TILE, 0:head_dim] = v_refs[j][:, cols]
            sj = lax.dot_general(q, k_refs[j][:, cols], (((1,), (1,)), ((), ())),
                                 preferred_element_type=F32)
            if j < KV_BLOCKS - 1:
                valid = t + (j - (KV_BLOCKS - 1)) >= 0
                off = pl.multiple_of(jnp.where(valid, j * Q_TILE, BAND_KEYS), Q_TILE)
            else:
                off = j * Q_TILE
            s.append(sj + bias_ref[h, :, pl.ds(off, Q_TILE)])
        m = jnp.max(jnp.maximum(jnp.maximum(s[0], s[1]), s[2]), axis=-1, keepdims=True)
        oe = None
        for j in range(KV_BLOCKS):
            pj = jnp.exp2(s[j] - m).astype(BF16)
            part = jnp.dot(pj, vext_ref[h, j * Q_TILE:(j + 1) * Q_TILE, :],
                           preferred_element_type=F32)
            oe = part if oe is None else oe + part
        y = oe[:, 0:head_dim] / oe[:, head_dim:]
        ss = ss + jnp.sum(y * y, axis=-1, keepdims=True)
        y_ref[:, cols] = y * (g_ref[:, cols] * _silu(z_ref[:, cols].astype(F32)))

    rinv = lax.rsqrt(ss * (1.0 / (n_heads * head_dim)) + EPS)
    for r0 in range(0, Q_TILE, ROW_CHUNK):
        rows = slice(r0, r0 + ROW_CHUNK)
        o_ref[rows, :] = (y_ref[rows, :] * rinv[rows, :]).astype(o_ref.dtype)


def _bias_rows(table):
    e = np.arange(BIAS_ROW)
    e = np.where(e >= BAND_KEYS, e - BIAS_ROW, e)
    idx = np.clip((KV_BLOCKS - 1) * Q_TILE - e, -MAX_REL, MAX_REL) + MAX_REL
    return table[:, idx].astype(F32) * LOG2_E


def _attention(proj, brow, g, *, batch, seq, width, n_heads):
    m = proj.shape[0]
    tiles = seq // Q_TILE
    head_dim = width // n_heads

    def spec(off, col):
        return pl.BlockSpec((Q_TILE, width),
                            lambda b, t: (b * tiles + jnp.maximum(t + off, 0), col))

    in_specs = [spec(0, 0),
                spec(-2, 1), spec(-1, 1), spec(0, 1),
                spec(-2, 2), spec(-1, 2), spec(0, 2),
                spec(0, 3),
                pl.BlockSpec(brow.shape, lambda b, t: (0, 0)),
                pl.BlockSpec((1, width), lambda b, t: (0, 0))]
    return pl.pallas_call(
        functools.partial(_attn_kernel, n_heads=n_heads, head_dim=head_dim),
        out_shape=jax.ShapeDtypeStruct((m, width), BF16),
        grid=(batch, tiles),
        in_specs=in_specs,
        out_specs=pl.BlockSpec((Q_TILE, width), lambda b, t: (b * tiles + t, 0)),
        scratch_shapes=[pltpu.VMEM((Q_TILE, width), F32),
                        pltpu.VMEM((n_heads, Q_TILE, BAND_KEYS + Q_TILE), F32),
                        pltpu.VMEM((n_heads, BAND_KEYS, 2 * head_dim), BF16)],
        compiler_params=_cparams(2),
        name="band_attention",
    )(*([proj] * 8), brow, g.reshape(1, width))


def _conv_kernel(a_ref, gl_ref, z_ref, wdw_ref, bdw_ref, lng_ref, lnb_ref, wpw_ref, bpw_ref, g_ref,
                 o_ref, u2_ref, c_ref, yb_ref, pw_ref, *, strip_rows):
    tt, width = a_ref.shape
    n_slabs = width // LANES
    t = pl.program_id(1)

    @pl.when(t == 0)
    def _():
        u2_ref[...] = jnp.zeros_like(u2_ref)

    @pl.when(t > 0)
    def _():
        u2_ref[:, 0:2 * HALO, :] = u2_ref[:, 2 * tt:2 * (tt + HALO), :]

    def glu_rows(rows):
        u = a_ref[rows, :].astype(F32) * (jnp.tanh(gl_ref[rows, :].astype(F32)) + 1.0)
        for c in range(n_slabs):
            u2_ref[c, pl.ds(2 * (HALO + rows.start), ROW_CHUNK, stride=2), :] = (
                u[:, c * LANES:(c + 1) * LANES])
    _row_loop(tt, glu_rows)

    def conv_slab(c, carry):
        cols = pl.ds(pl.multiple_of(c * LANES, LANES), LANES)
        for r0 in range(0, tt, strip_rows):
            acc = jnp.zeros((strip_rows, LANES), F32)
            for k in range(CONV_K):
                first = r0 + HALO - (CONV_K - 1) + k
                acc += u2_ref[c, pl.ds(2 * first, strip_rows, stride=2), :] * wdw_ref[k:k + 1, cols]
            c_ref[r0:r0 + strip_rows, cols] = acc + bdw_ref[:, cols]
        return carry
    lax.fori_loop(0, n_slabs, conv_slab, 0)

    def norm_rows(rows):
        c = c_ref[rows, :]
        mu = jnp.mean(c, axis=-1, keepdims=True)
        cc = c - mu
        var = jnp.mean(cc * cc, axis=-1, keepdims=True)
        h = cc * lax.rsqrt(var + EPS) * (0.5 * lng_ref[...]) + 0.5 * lnb_ref[...]
        yb_ref[rows, :] = (h * (jnp.tanh(h) + 1.0)).astype(BF16)
    _row_loop(tt, norm_rows, unroll=4)

    pw_ref[...] = jnp.dot(yb_ref[...], wpw_ref[...], preferred_element_type=F32)

    def finish_rows(rows):
        pw = pw_ref[rows, :] + bpw_ref[...]
        ms = jnp.mean(pw * pw, axis=-1, keepdims=True)
        z = z_ref[rows, :].astype(F32)
        o_ref[rows, :] = (pw * lax.rsqrt(ms + EPS) * g_ref[...] * _silu(z)).astype(o_ref.dtype)
    _row_loop(tt, finish_rows, unroll=2)


def _conv_module(proj, w_dw, b_dw, ln_g, ln_b, w_pw_bf, b_pw, g, *, batch, seq, width, col0, tt):
    m = proj.shape[0]
    tiles = seq // tt
    w_dw_p = jnp.zeros((HALO, width), F32).at[:CONV_K].set(w_dw)

    def spec(col):
        return pl.BlockSpec((tt, width), lambda b, t: (b * tiles + t, col))

    def vec():
        return pl.BlockSpec((1, width), lambda b, t: (0, 0))

    return pl.pallas_call(
        functools.partial(_conv_kernel, strip_rows=128),
        out_shape=jax.ShapeDtypeStruct((m, width), BF16),
        grid=(batch, tiles),
        in_specs=[spec(col0), spec(col0 + 1), spec(col0 + 2),
                  pl.BlockSpec((HALO, width), lambda b, t: (0, 0)),
                  vec(), vec(), vec(),
                  pl.BlockSpec((width, width), lambda b, t: (0, 0)),
                  vec(), vec()],
        out_specs=pl.BlockSpec((tt, width), lambda b, t: (b * tiles + t, 0)),
        scratch_shapes=[pltpu.VMEM((width // LANES, 2 * (tt + HALO), LANES), F32),
                        pltpu.VMEM((tt, width), F32),
                        pltpu.VMEM((tt, width), BF16),
                        pltpu.VMEM((tt, width), F32)],
        compiler_params=_cparams(2),
        name="conv_module",
    )(proj, proj, proj, w_dw_p, b_dw.reshape(1, width), ln_g.reshape(1, width),
      ln_b.reshape(1, width), w_pw_bf, b_pw.reshape(1, width), g.reshape(1, width))


def _cast_pair_kernel(a_ref, b_ref, o_ref):
    @pl.when(pl.program_id(0) == 0)
    def _():
        o_ref[...] = a_ref[...].astype(o_ref.dtype)

    @pl.when(pl.program_id(0) == 1)
    def _():
        o_ref[...] = b_ref[...].astype(o_ref.dtype)


def _cast_pair(a, b, *, rows):
    r, c = a.shape
    nr = r // rows
    return pl.pallas_call(
        _cast_pair_kernel,
        out_shape=jax.ShapeDtypeStruct((2, r, c), BF16),
        grid=(2, nr),
        in_specs=[pl.BlockSpec((rows, c), lambda s, t: (jnp.where(s == 0, t, nr - 1), 0)),
                  pl.BlockSpec((rows, c), lambda s, t: (jnp.where(s == 1, t, 0), 0))],
        out_specs=pl.BlockSpec((None, rows, c), lambda s, t: (s, t, 0)),
        compiler_params=_cparams(2),
        name="cast_weight_pair",
    )(a, b)


def _out_kernel(x_ref, ya_ref, yc_ref, w_ref, pg_ref, bg_ref, p_ref, wp_ref, fg_ref,
                o_ref, h_ref, hn_ref, ssh_ref, sso_ref, *, nj, tn, n_blocks, apply_final):
    i = pl.program_id(0)
    j = pl.program_id(1)
    cur = i % 2
    prev = 1 - cur
    wa = ya_ref.shape[1]
    d = hn_ref.shape[1]
    to = o_ref.shape[1]

    @pl.when((i < n_blocks) & (j == 0))
    def _():
        ssh_ref[...] = jnp.zeros_like(ssh_ref)
        sso_ref[cur] = jnp.zeros(sso_ref.shape[1:], F32)

    @pl.when((i < n_blocks) & (j < nj))
    def _():
        cols = pl.ds(pl.multiple_of(j * tn, tn), tn)
        acc = jnp.dot(ya_ref[...], w_ref[0:wa, :], preferred_element_type=F32)
        acc += jnp.dot(yc_ref[...], w_ref[wa:, :], preferred_element_type=F32)
        h = x_ref[...] + acc
        h_ref[cur, :, cols] = h
        hn_ref[:, cols] = (h * pg_ref[:, cols]).astype(BF16)
        ssh_ref[...] += jnp.sum(h * h, axis=-1, keepdims=True)

    @pl.when((i < n_blocks) & (j >= nj))
    def _():
        cols = pl.ds(pl.multiple_of((j - nj) * tn, tn), tn)
        half_rinv = 0.5 * lax.rsqrt(ssh_ref[...] * (1.0 / d) + EPS)
        gate2 = jnp.tanh(
            jnp.dot(hn_ref[...], w_ref[...], preferred_element_type=F32) * half_rinv
            + 0.5 * bg_ref[...]) + 1.0
        half_pe = jnp.dot((0.5 * p_ref[...]).astype(BF16), wp_ref[...], preferred_element_type=F32)
        h2 = h_ref[cur, :, cols] + gate2 * half_pe
        h_ref[cur, :, cols] = h2
        sso_ref[cur] += jnp.sum(h2 * h2, axis=-1, keepdims=True)

    @pl.when(i > 0)
    def _():
        ocols = pl.ds(pl.multiple_of(j * to, to), to)
        y = h_ref[prev, :, ocols]
        if apply_final:
            y = y * lax.rsqrt(sso_ref[prev] * (1.0 / d) + EPS) * fg_ref[:, ocols]
        o_ref[...] = y


def _out_block(x2, y_a, y_c, w_pair_bf, ple_g, b_gate, p2, w_ple_bf, final_g, *, tm, tn, apply_final):
    m, d = x2.shape
    wa = y_a.shape[1]
    wc = y_c.shape[1]
    pd = p2.shape[1]
    nj = d // tn
    nb = m // tm
    to = d // (2 * nj)

    def row(i):
        return jnp.minimum(i, nb - 1)

    def first(j):
        return jnp.minimum(j, nj - 1)

    def second(j):
        return jnp.maximum(j - nj, 0)

    def w_block(i, j):
        jj = jnp.where(i < nb, j, 2 * nj - 1)
        return (jj // nj, 0, jj % nj)

    return pl.pallas_call(
        functools.partial(_out_kernel, nj=nj, tn=tn, n_blocks=nb, apply_final=apply_final),
        out_shape=jax.ShapeDtypeStruct((m, d), F32),
        grid=(nb + 1, 2 * nj),
        in_specs=[
            pl.BlockSpec((tm, tn), lambda i, j: (row(i), first(j))),
            pl.BlockSpec((tm, wa), lambda i, j: (row(i), 0)),
            pl.BlockSpec((tm, wc), lambda i, j: (row(i), 0)),
            pl.BlockSpec((None, d, tn), w_block),
            pl.BlockSpec((1, d), lambda i, j: (0, 0)),
            pl.BlockSpec((1, tn), lambda i, j: (0, second(j))),
            pl.BlockSpec((tm, pd), lambda i, j: (row(i), 0)),
            pl.BlockSpec((pd, tn), lambda i, j: (0, second(j))),
            pl.BlockSpec((1, d), lambda i, j: (0, 0)),
        ],
        out_specs=pl.BlockSpec((tm, to),
                               lambda i, j: (jnp.maximum(i - 1, 0), jnp.where(i == 0, 0, j))),
        scratch_shapes=[pltpu.VMEM((2, tm, d), F32), pltpu.VMEM((tm, d), BF16),
                        pltpu.VMEM((tm, 1), F32), pltpu.VMEM((2, tm, 1), F32)],
        compiler_params=_cparams(2),
        name="out_block",
    )(x2, y_a, y_c, w_pair_bf, ple_g.reshape(1, d), b_gate.reshape(1, d), p2, w_ple_bf,
      final_g.reshape(1, d))


def kernel(x, p, norm_in_g, w_in, rel_table, w_dw, b_dw, conv_ln_g, conv_ln_b, w_pw, b_pw,
           attn_out_g, conv_out_g, w_out, ple_norm_g, w_ple_gate, b_ple_gate, w_ple, final_g):
    batch, seq, d = x.shape
    depth = w_in.shape[0]
    conv_width = w_pw.shape[1]
    att_width = d - conv_width
    m = batch * seq
    assert depth >= 1
    assert seq % Q_TILE == 0 and Q_TILE % CHUNK == 0
    assert (KV_BLOCKS - 1) * Q_TILE == LEFT_CHUNKS * CHUNK
    assert att_width == conv_width
    head_dim = att_width // N_HEADS
    col = jnp.arange(w_in.shape[2])
    col_scale = jnp.where(col < att_width, head_dim ** -0.5 * LOG2_E,
                          jnp.where((col >= 4 * att_width) & (col < 4 * att_width + 2 * conv_width),
                                    0.5, 1.0)).astype(F32)

    h = x.reshape(m, d)
    for i in range(depth):
        last = i == depth - 1
        proj = _in_projection(h, norm_in_g[i], (w_in[i] * col_scale).astype(BF16),
                              tm=min(1024, m), tn=min(1024, d))
        y_a = _attention(proj, _bias_rows(rel_table[i]), attn_out_g[i],
                         batch=batch, seq=seq, width=att_width, n_heads=N_HEADS)
        y_c = _conv_module(proj, w_dw[i], b_dw[i], conv_ln_g[i], conv_ln_b[i],
                           w_pw[i].astype(BF16), b_pw[i], conv_out_g[i],
                           batch=batch, seq=seq, width=conv_width, col0=4, tt=min(256, seq))
        w_pair = _cast_pair(w_out[i], w_ple_gate[i], rows=256)
        h = _out_block(h, y_a, y_c, w_pair, ple_norm_g[i], b_ple_gate[i], p[i].reshape(m, -1),
                       w_ple[i].astype(BF16), final_g, tm=min(512, m), tn=min(1024, d),
                       apply_final=last)
    return h.reshape(batch, seq, d)
```
